```python
import math
import jax, jax.numpy as jnp
from jax import lax
import numpy as np

D_MODEL = 1024
BATCH = 4
SEQ = 4096
DEPTH = 4
DEC_BATCH = 32
DEC_SEQ = 8
PAST_LEN = 8192
PAGE_SIZE = 128

N_HEADS = 8
QK_DIM = D_MODEL // 16
V_DIM = 2 * QK_DIM
D_ATTN = N_HEADS * V_DIM
ROPE_DIM = QK_DIM // 4
ROPE_THETA = 500000.0
Q_BLOCK = 128
D_RNN = (D_MODEL * 5) // 4
RG_BLOCK = 128
N_RG_BLOCKS = D_RNN // RG_BLOCK
CONV_W = 4
RG_C = 8.0
N_GROUPS = 4
EXPERTS_PER_GROUP = 8
N_EXPERTS = N_GROUPS * EXPERTS_PER_GROUP
TOP_K = 2
D_EXPERT = D_MODEL // 2
MOE_BLOCK = 128
PLE_DIM = 256
N_ATTN = (DEPTH + 1) // 2
N_LRU = DEPTH // 2
EPS = 1e-6

kernel_name = "hybrid_diffattn_rglru_hmoe_step"


def rmsnorm(x, g):
    xf = x.astype(jnp.float32)
    y = xf * lax.rsqrt(jnp.mean(xf * xf, axis=-1, keepdims=True) + EPS)
    return (y * g.astype(jnp.float32)).astype(x.dtype)


def rope_partial(x, pos):
    half = ROPE_DIM // 2
    inv = ROPE_THETA ** (-(jnp.arange(half, dtype=jnp.float32) * 2.0 / ROPE_DIM))
    ang = pos.astype(jnp.float32)[:, None] * inv[None, :]
    cos = jnp.cos(ang)[None, :, None, None, :]
    sin = jnp.sin(ang)[None, :, None, None, :]
    xr = x[..., :ROPE_DIM].astype(jnp.float32)
    x1, x2 = xr[..., :half], xr[..., half:]
    rot = jnp.concatenate([x1 * cos - x2 * sin, x2 * cos + x1 * sin], axis=-1).astype(x.dtype)
    return jnp.concatenate([rot, x[..., ROPE_DIM:]], axis=-1)


def diff_attend(q, k, v, q_pos, k_pos, lam):
    B, Sq = q.shape[0], q.shape[1]
    qb = min(Q_BLOCK, Sq)
    nb = -(-Sq // qb)
    pad = nb * qb - Sq
    q = jnp.pad(q, ((0, 0), (0, pad), (0, 0), (0, 0), (0, 0)))
    q_pos = jnp.pad(q_pos, (0, pad), mode='edge')
    qs = q.reshape(B, nb, qb, N_HEADS, 2, QK_DIM).transpose(1, 0, 2, 3, 4, 5)
    ps = q_pos.reshape(nb, qb)
    scale = QK_DIM ** -0.5

    def block(args):
        qi, pi = args
        s = jnp.einsum('bqhcd,bkhcd->bhcqk', qi, k).astype(jnp.float32) * scale
        mask = pi[:, None] >= k_pos[None, :]
        s = jnp.where(mask[None, None, None], s, -jnp.inf)
        pr = jax.nn.softmax(s, axis=-1)
        a = pr[:, :, 0] - lam * pr[:, :, 1]
        return jnp.einsum('bhqk,bkhd->bqhd', a.astype(v.dtype), v)

    o = lax.map(block, (qs, ps))
    o = o.transpose(1, 0, 2, 3, 4).reshape(B, nb * qb, N_HEADS, V_DIM)
    return o[:, :Sq]


def diff_attn_mixer(hn, pos, w_qkv, q_gain, k_gain, lq1, lk1, lq2, lk2, sub_gain, w_o,
                    lambda_init, k_past, v_past, past_pos):
    B, S, _ = hn.shape
    qkv = hn @ w_qkv
    q, k, v = jnp.split(qkv, 3, axis=-1)
    q = rope_partial(rmsnorm(q.reshape(B, S, N_HEADS, 2, QK_DIM), q_gain), pos)
    k = rope_partial(rmsnorm(k.reshape(B, S, N_HEADS, 2, QK_DIM), k_gain), pos)
    v = v.reshape(B, S, N_HEADS, V_DIM)
    if k_past is None:
        k_all, v_all, k_pos = k, v, pos
    else:
        k_all = jnp.concatenate([k_past.astype(k.dtype), k], axis=1)
        v_all = jnp.concatenate([v_past.astype(v.dtype), v], axis=1)
        k_pos = jnp.concatenate([past_pos, pos])
    f32 = jnp.float32
    lam = (jnp.exp(jnp.sum(lq1.astype(f32) * lk1.astype(f32)))
           - jnp.exp(jnp.sum(lq2.astype(f32) * lk2.astype(f32))) + lambda_init)
    o = diff_attend(q, k_all, v_all, pos, k_pos, lam)
    o = rmsnorm(o, sub_gain) * (1.0 - lambda_init)
    return o.reshape(B, S, D_ATTN) @ w_o, k, v


def rglru_mixer(hn, w_in, conv_w, conv_b, w_a, b_a, w_i, b_i, lam_param, w_out, buf0, h0):
    B, S, _ = hn.shape
    f32 = jnp.float32
    u = hn @ w_in
    gate, xr = u[..., :D_RNN], u[..., D_RNN:]
    xp = jnp.concatenate([buf0.astype(xr.dtype), xr], axis=1)
    xpf = xp.astype(f32)
    cw = conv_w.astype(f32)
    xc = conv_b.astype(f32) + sum(cw[j] * xpf[:, j:j + S] for j in range(CONV_W))
    new_buf = xp[:, -(CONV_W - 1):]
    xb = xc.reshape(B, S, N_RG_BLOCKS, RG_BLOCK)
    r = jax.nn.sigmoid(jnp.einsum('bsnc,ncd->bsnd', xb, w_a.astype(f32)).reshape(B, S, D_RNN) + b_a.astype(f32))
    ig = jax.nn.sigmoid(jnp.einsum('bsnc,ncd->bsnd', xb, w_i.astype(f32)).reshape(B, S, D_RNN) + b_i.astype(f32))
    log_a = -RG_C * r * jax.nn.softplus(-lam_param.astype(f32))
    a = jnp.exp(log_a)
    mult = jnp.sqrt(-jnp.expm1(2.0 * log_a))
    b = mult * (ig * xc)
    b = b.at[:, 0].add(a[:, 0] * h0.astype(f32))

    def combine(c1, c2):
        a1, b1 = c1
        a2, b2 = c2
        return a1 * a2, a2 * b1 + b2

    _, hs = lax.associative_scan(combine, (a, b), axis=1)
    y = (hs * jax.nn.gelu(gate.astype(f32))).astype(hn.dtype)
    return y @ w_out, new_buf, hs[:, -1].astype(hn.dtype)


def hier_moe(hn, w_group, b_group, w_sub, b_sub, w_gate_up, w_down):
    B, S, D = hn.shape
    T = B * S
    f32 = jnp.float32
    xt = hn.reshape(T, D)
    gl = (xt @ w_group).astype(f32) + b_group.astype(f32)
    gp = jax.nn.softmax(gl, axis=-1)
    g_idx = jnp.argmax(gl, axis=-1)
    g_w = jnp.take_along_axis(gp, g_idx[:, None], axis=-1)
    sl = ((xt @ w_sub).astype(f32) + b_sub.astype(f32)).reshape(T, N_GROUPS, EXPERTS_PER_GROUP)
    sl = jnp.take_along_axis(sl, g_idx[:, None, None], axis=1)[:, 0]
    sp = jax.nn.softmax(sl, axis=-1)
    top_v, top_i = lax.top_k(sp, TOP_K)
    top_v = top_v / jnp.sum(top_v, axis=-1, keepdims=True)
    eid = (g_idx[:, None] * EXPERTS_PER_GROUP + top_i).reshape(T * TOP_K)
    wts = (g_w * top_v).reshape(T * TOP_K)
    N = T * TOP_K
    order = jnp.argsort(eid)
    se = eid[order]
    tok = order // TOP_K
    counts = jnp.zeros((N_EXPERTS,), jnp.int32).at[eid].add(1)
    starts = jnp.cumsum(counts) - counts
    pcounts = ((counts + MOE_BLOCK - 1) // MOE_BLOCK) * MOE_BLOCK
    pends = jnp.cumsum(pcounts)
    pstarts = pends - pcounts
    dest = pstarts[se] + (jnp.arange(N, dtype=jnp.int32) - starts[se])
    n_blk = -(-N // MOE_BLOCK) + N_EXPERTS
    xs = jnp.zeros((n_blk * MOE_BLOCK, D), xt.dtype).at[dest].set(xt[tok])
    blk_e = jnp.clip(jnp.searchsorted(pends, jnp.arange(n_blk, dtype=jnp.int32) * MOE_BLOCK, side='right'),
                     0, N_EXPERTS - 1)

    def expert_block(args):
        xb, e = args
        gu = xb @ w_gate_up[e]
        return (jax.nn.silu(gu[:, :D_EXPERT]) * gu[:, D_EXPERT:]) @ w_down[e]

    ys = lax.map(expert_block, (xs.reshape(n_blk, MOE_BLOCK, D), blk_e)).reshape(n_blk * MOE_BLOCK, D)
    contrib = ys[dest].astype(f32) * wts[order][:, None]
    out = jnp.zeros((T, D), f32).at[tok].add(contrib)
    return out.astype(hn.dtype).reshape(B, S, D)


def setup_inputs(seed: int = 0) -> dict:
    key = jax.random.key(seed)
    ks = iter(jax.random.split(key, 48))
    f32 = jnp.float32

    def nrm(shape, scale):
        return jax.random.normal(next(ks), shape, f32) * scale

    def gain(shape):
        return 1.0 + 0.02 * jax.random.normal(next(ks), shape, f32)

    n_pages = PAST_LEN // PAGE_SIZE
    n_used = DEC_BATCH * n_pages
    n_pool = n_used + max(1, n_used // 4)
    x_prompt = nrm((BATCH, SEQ, D_MODEL), 1.0)
    x_sample = nrm((DEC_BATCH, DEC_SEQ, D_MODEL), 1.0)
    cache_k = nrm((N_ATTN, n_pool, PAGE_SIZE, N_HEADS, 2, QK_DIM), 1.0)
    cache_v = nrm((N_ATTN, n_pool, PAGE_SIZE, N_HEADS, V_DIM), 1.0)
    state_h = nrm((N_LRU, DEC_BATCH, D_RNN), 0.5)
    state_conv = nrm((N_LRU, DEC_BATCH, CONV_W - 1, D_RNN), 1.0)
    page_table = jax.random.permutation(next(ks), n_pool)[:n_used].reshape(DEC_BATCH, n_pages).astype(jnp.int32)
    p_prompt = nrm((DEPTH, BATCH, SEQ, PLE_DIM), 1.0)
    p_sample = nrm((DEPTH, DEC_BATCH, DEC_SEQ, PLE_DIM), 1.0)
    u = jax.random.uniform(next(ks), (N_LRU, D_RNN), f32, minval=0.9, maxval=0.999)
    a0 = u ** (1.0 / RG_C)
    lru_lambda = jnp.log(a0) - jnp.log1p(-a0)
    return {
        "x_prompt": x_prompt, "x_sample": x_sample,
        "cache_k": cache_k, "cache_v": cache_v,
        "state_h": state_h, "state_conv": state_conv,
        "page_table": page_table,
        "p_prompt": p_prompt, "p_sample": p_sample,
        "mix_norm": gain((DEPTH, D_MODEL)),
        "ffn_norm": gain((DEPTH, D_MODEL)),
        "ple_norm": gain((DEPTH, D_MODEL)),
        "w_qkv": nrm((N_ATTN, D_MODEL, 3 * D_ATTN), D_MODEL ** -0.5),
        "q_norm": gain((N_ATTN, QK_DIM)),
        "k_norm": gain((N_ATTN, QK_DIM)),
        "lambda_q1": nrm((N_ATTN, QK_DIM), 0.1),
        "lambda_k1": nrm((N_ATTN, QK_DIM), 0.1),
        "lambda_q2": nrm((N_ATTN, QK_DIM), 0.1),
        "lambda_k2": nrm((N_ATTN, QK_DIM), 0.1),
        "sub_norm": gain((N_ATTN, V_DIM)),
        "w_o": nrm((N_ATTN, D_ATTN, D_MODEL), D_ATTN ** -0.5),
        "w_in_lru": nrm((N_LRU, D_MODEL, 2 * D_RNN), D_MODEL ** -0.5),
        "conv_w": nrm((N_LRU, CONV_W, D_RNN), CONV_W ** -0.5),
        "conv_b": nrm((N_LRU, D_RNN), 0.01),
        "w_rg_a": nrm((N_LRU, N_RG_BLOCKS, RG_BLOCK, RG_BLOCK), RG_BLOCK ** -0.5),
        "b_rg_a": nrm((N_LRU, D_RNN), 0.01),
        "w_rg_i": nrm((N_LRU, N_RG_BLOCKS, RG_BLOCK, RG_BLOCK), RG_BLOCK ** -0.5),
        "b_rg_i": nrm((N_LRU, D_RNN), 0.01),
        "lru_lambda": lru_lambda,
        "w_out_lru": nrm((N_LRU, D_RNN, D_MODEL), D_RNN ** -0.5),
        "w_group": nrm((DEPTH, D_MODEL, N_GROUPS), D_MODEL ** -0.5),
        "b_group": nrm((DEPTH, N_GROUPS), 0.01),
        "w_sub": nrm((DEPTH, D_MODEL, N_EXPERTS), D_MODEL ** -0.5),
        "b_sub": nrm((DEPTH, N_EXPERTS), 0.01),
        "w_gate_up": nrm((DEPTH, N_EXPERTS, D_MODEL, 2 * D_EXPERT), D_MODEL ** -0.5),
        "w_down": nrm((DEPTH, N_EXPERTS, D_EXPERT, D_MODEL), D_EXPERT ** -0.5),
        "w_ple_gate": nrm((DEPTH, D_MODEL, D_MODEL), D_MODEL ** -0.5),
        "w_ple_proj": nrm((DEPTH, PLE_DIM, D_MODEL), PLE_DIM ** -0.5),
    }


def reference(x_prompt, x_sample, cache_k, cache_v, state_h, state_conv, page_table, p_prompt, p_sample,
              mix_norm, ffn_norm, ple_norm, w_qkv, q_norm, k_norm, lambda_q1, lambda_k1, lambda_q2, lambda_k2,
              sub_norm, w_o, w_in_lru, conv_w, conv_b, w_rg_a, b_rg_a, w_rg_i, b_rg_i, lru_lambda, w_out_lru,
              w_group, b_group, w_sub, b_sub, w_gate_up, w_down, w_ple_gate, w_ple_proj):
    past_len = page_table.shape[1] * PAGE_SIZE

    def trunk(h, p, pos, sample):
        nb = h.shape[0]
        ks, vs, hl, bl = [], [], [], []
        for i in range(DEPTH):
            hn = rmsnorm(h, mix_norm[i])
            j = i // 2
            if i % 2 == 0:
                lambda_init = 0.8 - 0.6 * math.exp(-0.3 * i)
                if sample:
                    kp = cache_k[j, page_table].reshape(nb, past_len, N_HEADS, 2, QK_DIM)
                    vp = cache_v[j, page_table].reshape(nb, past_len, N_HEADS, V_DIM)
                    pp = jnp.arange(past_len, dtype=jnp.int32)
                else:
                    kp, vp, pp = None, None, None
                out, kn, vn = diff_attn_mixer(hn, pos, w_qkv[j], q_norm[j], k_norm[j], lambda_q1[j], lambda_k1[j],
                                              lambda_q2[j], lambda_k2[j], sub_norm[j], w_o[j], lambda_init,
                                              kp, vp, pp)
                ks.append(kn)
                vs.append(vn)
            else:
                if sample:
                    buf0, h0 = state_conv[j], state_h[j]
                else:
                    buf0 = jnp.zeros((nb, CONV_W - 1, D_RNN), h.dtype)
                    h0 = jnp.zeros((nb, D_RNN), h.dtype)
                out, bn, hn_last = rglru_mixer(hn, w_in_lru[j], conv_w[j], conv_b[j], w_rg_a[j], b_rg_a[j],
                                               w_rg_i[j], b_rg_i[j], lru_lambda[j], w_out_lru[j], buf0, h0)
                bl.append(bn)
                hl.append(hn_last)
            h = h + out
            h = h + hier_moe(rmsnorm(h, ffn_norm[i]), w_group[i], b_group[i], w_sub[i], b_sub[i],
                             w_gate_up[i], w_down[i])
            g = jax.nn.sigmoid(rmsnorm(h, ple_norm[i]) @ w_ple_gate[i])
            h = h + g * (p[i] @ w_ple_proj[i])
        return h, jnp.stack(ks), jnp.stack(vs), jnp.stack(hl), jnp.stack(bl)

    pos_prompt = jnp.arange(x_prompt.shape[1], dtype=jnp.int32)
    pos_sample = past_len + jnp.arange(x_sample.shape[1], dtype=jnp.int32)
    y_prompt, k_prompt, v_prompt, h_prompt, conv_prompt = trunk(x_prompt, p_prompt, pos_prompt, False)
    y_sample, k_sample, v_sample, h_sample, conv_sample = trunk(x_sample, p_sample, pos_sample, True)
    return (y_prompt, y_sample, k_prompt, v_prompt, h_prompt, conv_prompt,
            k_sample, v_sample, h_sample, conv_sample)
```

```python
import functools
import math

import jax
import jax.numpy as jnp
import numpy as np
from jax import lax
from jax.experimental import pallas as pl
from jax.experimental.pallas import tpu as pltpu

F32 = jnp.float32
BF16 = jnp.bfloat16

D_MODEL = 1024
DEPTH = 4
PAGE_SIZE = 128
N_HEADS = 8
QK_DIM = 64
V_DIM = 128
ROPE_DIM = 16
ROPE_THETA = 500000.0
D_RNN = 1280
RG_BLOCK = 128
N_RG_BLOCKS = 10
CONV_W = 4
RG_C = 8.0
N_GROUPS = 4
EXPERTS_PER_GROUP = 8
N_EXPERTS = 32
TOP_K = 2
D_EXPERT = 512
PLE_DIM = 256
EPS = 1e-6

TM = 256
TQ = 256
PAGES_PER_STEP = 4
MOE_TM = 256
LRU_TS = 256
VMEM_LIMIT = 56 * 1024 * 1024

_NT = (((1,), (1,)), ((), ()))


def _cparams(sem):
    return pltpu.CompilerParams(dimension_semantics=sem, vmem_limit_bytes=VMEM_LIMIT)


def _rms(x, g):
    return x * lax.rsqrt(jnp.mean(x * x, axis=-1, keepdims=True) + EPS) * g


def _tile(i, n_p):
    return jnp.where(i == 0, n_p, i - 1)


def _ptile(i):
    return jnp.maximum(i - 1, 0)


def _qkv_kernel(h_ref, g_ref, wq_ref, wk_ref, wv_ref, qg_ref, kg_ref, cos_ref, sin_ref,
                q_ref, ktp_ref, kts_ref, ktb_ref, vp_ref, vs_ref, vb_ref, qt_s, kt_s):
    i = pl.program_id(0)
    hn = _rms(h_ref[...], g_ref[...]).astype(BF16)
    cos = cos_ref[...]
    sin = sin_ref[...]

    def normrope(w_ref, gain_ref, dst, scale):
        xt = lax.dot_general(w_ref[...], hn, _NT, preferred_element_type=F32)
        gain = gain_ref[...]
        for g in range(2 * N_HEADS):
            xg = xt[g * QK_DIM:(g + 1) * QK_DIM, :]
            ss = jnp.sum(xg * xg, axis=0, keepdims=True)
            yg = xg * lax.rsqrt(ss * (1.0 / QK_DIM) + EPS) * gain
            y1 = yg[0:8]
            y2 = yg[8:16]
            dst[g * QK_DIM:g * QK_DIM + 8, :] = (y1 * cos - y2 * sin) * scale
            dst[g * QK_DIM + 8:g * QK_DIM + 16, :] = (y2 * cos + y1 * sin) * scale
            dst[g * QK_DIM + 16:(g + 1) * QK_DIM, :] = yg[16:] * scale

    normrope(wq_ref, qg_ref, qt_s, QK_DIM ** -0.5)
    normrope(wk_ref, kg_ref, kt_s, 1.0)
    q_ref[...] = qt_s[...].T
    v = jnp.dot(hn, wv_ref[...], preferred_element_type=F32)

    @pl.when(i == 0)
    def _():
        kts_ref[...] = kt_s[...]
        vs_ref[...] = v

    @pl.when(i > 0)
    def _():
        kt = kt_s[...]
        ktp_ref[...] = kt
        ktb_ref[...] = kt.astype(BF16)
        vp_ref[...] = v
        vb_ref[...] = v.astype(BF16)


def _qkv(h, g, wq_t, wk_t, wv, qg_b, kg_b, cos_t, sin_t, n_p, batch, seq):
    t_all = h.shape[0]
    n_sb = seq // TM
    const = lambda i: (0, 0)
    tile = lambda i: (_tile(i, n_p), 0)
    ktile = lambda i: (_ptile(i) // n_sb, 0, _ptile(i) % n_sb)
    return pl.pallas_call(
        _qkv_kernel,
        grid=(n_p + 1,),
        in_specs=[
            pl.BlockSpec((TM, D_MODEL), tile),
            pl.BlockSpec((1, D_MODEL), const),
            pl.BlockSpec((D_MODEL, D_MODEL), const),
            pl.BlockSpec((D_MODEL, D_MODEL), const),
            pl.BlockSpec((D_MODEL, D_MODEL), const),
            pl.BlockSpec((QK_DIM, TM), const),
            pl.BlockSpec((QK_DIM, TM), const),
            pl.BlockSpec((8, TM), lambda i: (0, _tile(i, n_p))),
            pl.BlockSpec((8, TM), lambda i: (0, _tile(i, n_p))),
        ],
        out_specs=[
            pl.BlockSpec((TM, D_MODEL), tile),
            pl.BlockSpec((None, D_MODEL, TM), ktile),
            pl.BlockSpec((D_MODEL, TM), const),
            pl.BlockSpec((None, D_MODEL, TM), ktile),
            pl.BlockSpec((TM, D_MODEL), lambda i: (_ptile(i), 0)),
            pl.BlockSpec((TM, D_MODEL), const),
            pl.BlockSpec((TM, D_MODEL), lambda i: (_ptile(i), 0)),
        ],
        out_shape=[
            jax.ShapeDtypeStruct((t_all, D_MODEL), F32),
            jax.ShapeDtypeStruct((batch, D_MODEL, seq), F32),
            jax.ShapeDtypeStruct((D_MODEL, TM), F32),
            jax.ShapeDtypeStruct((batch, D_MODEL, seq), BF16),
            jax.ShapeDtypeStruct((n_p * TM, D_MODEL), F32),
            jax.ShapeDtypeStruct((TM, D_MODEL), F32),
            jax.ShapeDtypeStruct((n_p * TM, D_MODEL), BF16),
        ],
        scratch_shapes=[pltpu.VMEM((D_MODEL, TM), F32), pltpu.VMEM((D_MODEL, TM), F32)],
        compiler_params=_cparams(("arbitrary",)),
        name="qkv",
    )(h, g, wq_t, wk_t, wv, qg_b, kg_b, cos_t, sin_t)


def _resid_matmul_kernel(h_ref, xp_ref, xs_ref, w_ref, o_ref):
    i = pl.program_id(0)
    x = jnp.where(i == 0, xs_ref[...], xp_ref[...]).astype(BF16)
    o_ref[...] = h_ref[...] + jnp.dot(x, w_ref[...], preferred_element_type=F32)


def _resid_matmul(h, x_p, x_s, w, n_p):
    t_all = h.shape[0]
    k = w.shape[0]
    tile = lambda i: (_tile(i, n_p), 0)
    return pl.pallas_call(
        _resid_matmul_kernel,
        grid=(n_p + 1,),
        in_specs=[
            pl.BlockSpec((TM, D_MODEL), tile),
            pl.BlockSpec((TM, k), lambda i: (_ptile(i), 0)),
            pl.BlockSpec((TM, k), lambda i: (0, 0)),
            pl.BlockSpec((k, D_MODEL), lambda i: (0, 0)),
        ],
        out_specs=pl.BlockSpec((TM, D_MODEL), tile),
        out_shape=jax.ShapeDtypeStruct((t_all, D_MODEL), F32),
        compiler_params=_cparams(("arbitrary",)),
        name="resid_matmul",
    )(h, x_p, x_s, w)


def _norm_matmul_kernel(h_ref, g_ref, w_ref, o_ref):
    hn = _rms(h_ref[...], g_ref[...]).astype(BF16)
    o_ref[...] = jnp.dot(hn, w_ref[...], preferred_element_type=F32)


def _norm_matmul(h, g, w):
    t_all = h.shape[0]
    n = w.shape[1]
    return pl.pallas_call(
        _norm_matmul_kernel,
        grid=(t_all // TM,),
        in_specs=[
            pl.BlockSpec((TM, D_MODEL), lambda i: (i, 0)),
            pl.BlockSpec((1, D_MODEL), lambda i: (0, 0)),
            pl.BlockSpec((D_MODEL, n), lambda i: (0, 0)),
        ],
        out_specs=pl.BlockSpec((TM, n), lambda i: (i, 0)),
        out_shape=jax.ShapeDtypeStruct((t_all, n), F32),
        compiler_params=_cparams(("arbitrary",)),
        name="norm_matmul",
    )(h, g, w)


def _router_kernel(h_ref, g_ref, whi_ref, wlo_ref, hn_ref, lg_ref):
    hn = _rms(h_ref[...], g_ref[...])
    hi = hn.astype(BF16)
    lo = (hn - hi.astype(F32)).astype(BF16)
    whi = whi_ref[...]
    lg = jnp.dot(hi, whi, preferred_element_type=F32)
    lg += jnp.dot(lo, whi, preferred_element_type=F32)
    lg += jnp.dot(hi, wlo_ref[...], preferred_element_type=F32)
    hn_ref[...] = hi
    lg_ref[...] = lg


def _router(h, g, w_hi, w_lo):
    t_all = h.shape[0]
    return pl.pallas_call(
        _router_kernel,
        grid=(t_all // TM,),
        in_specs=[
            pl.BlockSpec((TM, D_MODEL), lambda i: (i, 0)),
            pl.BlockSpec((1, D_MODEL), lambda i: (0, 0)),
            pl.BlockSpec((D_MODEL, 128), lambda i: (0, 0)),
            pl.BlockSpec((D_MODEL, 128), lambda i: (0, 0)),
        ],
        out_specs=[
            pl.BlockSpec((TM, D_MODEL), lambda i: (i, 0)),
            pl.BlockSpec((TM, 128), lambda i: (i, 0)),
        ],
        out_shape=[
            jax.ShapeDtypeStruct((t_all, D_MODEL), BF16),
            jax.ShapeDtypeStruct((t_all, 128), F32),
        ],
        compiler_params=_cparams(("arbitrary",)),
        name="router",
    )(h, g, w_hi, w_lo)


def _moe_ffn_kernel(be_ref, nv_ref, x_ref, w_ref, wgu_ref, wd_ref, o_ref):
    j = pl.program_id(0)

    @pl.when(j < nv_ref[0])
    def _():
        gu = jnp.dot(x_ref[...], wgu_ref[...], preferred_element_type=F32)
        act = jax.nn.silu(gu[:, :D_EXPERT]) * gu[:, D_EXPERT:]
        y = jnp.dot(act.astype(BF16), wd_ref[...], preferred_element_type=F32)
        o_ref[...] = y * w_ref[...]

    @pl.when(j >= nv_ref[0])
    def _():
        o_ref[...] = jnp.zeros_like(o_ref)


def _moe_ffn(blk_e, n_valid, xs, w_row, wgu, wd):
    n_rows = xs.shape[0]
    n_blk = n_rows // MOE_TM
    return pl.pallas_call(
        _moe_ffn_kernel,
        grid_spec=pltpu.PrefetchScalarGridSpec(
            num_scalar_prefetch=2,
            grid=(n_blk,),
            in_specs=[
                pl.BlockSpec((MOE_TM, D_MODEL), lambda j, be, nv: (j, 0)),
                pl.BlockSpec((MOE_TM, 1), lambda j, be, nv: (j, 0)),
                pl.BlockSpec((None, D_MODEL, 2 * D_EXPERT), lambda j, be, nv: (be[j], 0, 0)),
                pl.BlockSpec((None, D_EXPERT, D_MODEL), lambda j, be, nv: (be[j], 0, 0)),
            ],
            out_specs=pl.BlockSpec((MOE_TM, D_MODEL), lambda j, be, nv: (j, 0)),
        ),
        out_shape=jax.ShapeDtypeStruct((n_rows, D_MODEL), F32),
        compiler_params=_cparams(("arbitrary",)),
        name="moe_ffn",
    )(blk_e, n_valid, xs, w_row, wgu, wd)


def _ple_kernel(h_ref, ma_ref, mb_ref, g_ref, wg_ref, pp_ref, ps_ref, wp_ref, *o_refs, split):
    i = pl.program_id(0)
    x = h_ref[...] + (ma_ref[...] + mb_ref[...])
    hn = _rms(x, g_ref[...]).astype(BF16)
    gate = jax.nn.sigmoid(jnp.dot(hn, wg_ref[...], preferred_element_type=F32))
    p = jnp.where(i == 0, ps_ref[...], pp_ref[...]).astype(BF16)
    out = x + gate * jnp.dot(p, wp_ref[...], preferred_element_type=F32)
    if split:
        yp_ref, ys_ref = o_refs

        @pl.when(i == 0)
        def _():
            ys_ref[...] = out

        @pl.when(i > 0)
        def _():
            yp_ref[...] = out
    else:
        o_refs[0][...] = out


def _ple(h, ma, mb, g, wg, p_p, p_s, wp, layer, n_p, split):
    t_all = h.shape[0]
    tile = lambda i: (_tile(i, n_p), 0)
    const = lambda i: (0, 0)
    if split:
        out_specs = [pl.BlockSpec((TM, D_MODEL), lambda i: (_ptile(i), 0)),
                     pl.BlockSpec((TM, D_MODEL), const)]
        out_shape = [jax.ShapeDtypeStruct((n_p * TM, D_MODEL), F32),
                     jax.ShapeDtypeStruct((TM, D_MODEL), F32)]
    else:
        out_specs = pl.BlockSpec((TM, D_MODEL), tile)
        out_shape = jax.ShapeDtypeStruct((t_all, D_MODEL), F32)
    return pl.pallas_call(
        functools.partial(_ple_kernel, split=split),
        grid=(n_p + 1,),
        in_specs=[
            pl.BlockSpec((TM, D_MODEL), tile),
            pl.BlockSpec((TM, D_MODEL), tile),
            pl.BlockSpec((TM, D_MODEL), tile),
            pl.BlockSpec((1, D_MODEL), const),
            pl.BlockSpec((D_MODEL, D_MODEL), const),
            pl.BlockSpec((None, TM, PLE_DIM), lambda i: (layer, _ptile(i), 0)),
            pl.BlockSpec((None, TM, PLE_DIM), lambda i: (layer, 0, 0)),
            pl.BlockSpec((PLE_DIM, D_MODEL), const),
        ],
        out_specs=out_specs,
        out_shape=out_shape,
        compiler_params=_cparams(("arbitrary",)),
        name="ple",
    )(h, ma, mb, g, wg, p_p, p_s, wp)


def _lambda(lp_ref, lambda_init):
    lp = lp_ref[...]
    s1 = jnp.sum(lp[0:1] * lp[1:2], axis=-1, keepdims=True)
    s2 = jnp.sum(lp[2:3] * lp[3:4], axis=-1, keepdims=True)
    return jnp.exp(s1) - jnp.exp(s2) + lambda_init


def _head_out(o0, o1, lam, sg, lambda_init):
    o = o0 - lam * o1
    return _rms(o, sg) * (1.0 - lambda_init)


def _flash_kernel(q_ref, kt_ref, v_ref, lp_ref, sg_ref, o_ref, m_s, l_s, acc_s, *, lambda_init):
    qi = pl.program_id(2)
    q = q_ref[...].astype(BF16)
    lane = lax.broadcasted_iota(jnp.int32, q.shape, 1)
    zero = jnp.zeros_like(q)
    q2 = jnp.concatenate([jnp.where(lane < QK_DIM, q, zero), jnp.where(lane >= QK_DIM, q, zero)], axis=0)
    m_s[...] = jnp.full_like(m_s, -jnp.inf)
    l_s[...] = jnp.zeros_like(l_s)
    acc_s[...] = jnp.zeros_like(acc_s)

    def attend(j, masked):
        c0 = pl.multiple_of(j * TQ, TQ)
        kt = kt_ref[:, pl.ds(c0, TQ)]
        vj = v_ref[pl.ds(c0, TQ), :]
        s = jnp.dot(q2, kt, preferred_element_type=F32)
        if masked:
            row = lax.broadcasted_iota(jnp.int32, s.shape, 0)
            col = lax.broadcasted_iota(jnp.int32, s.shape, 1)
            tok = jnp.where(row >= TQ, row - TQ, row)
            s = jnp.where(col <= tok, s, -jnp.inf)
        m_prev = m_s[...]
        m_new = jnp.maximum(m_prev, jnp.max(s, axis=-1, keepdims=True))
        alpha = jnp.exp(m_prev - m_new)
        p = jnp.exp(s - m_new)
        l_s[...] = alpha * l_s[...] + jnp.sum(p, axis=-1, keepdims=True)
        acc_s[...] = alpha * acc_s[...] + jnp.dot(p.astype(BF16), vj, preferred_element_type=F32)
        m_s[...] = m_new

    def body(j, carry):
        attend(j, False)
        return carry

    lax.fori_loop(0, qi, body, 0)
    attend(qi, True)

    on = acc_s[...] / l_s[...]
    lam = _lambda(lp_ref, lambda_init)
    o_ref[...] = _head_out(on[:TQ], on[TQ:], lam, sg_ref[...], lambda_init)


def _flash(q, kt_b, v_b, lam_p, sub_g, batch, seq, lambda_init):
    n_q = seq // TQ
    return pl.pallas_call(
        functools.partial(_flash_kernel, lambda_init=lambda_init),
        grid=(batch, N_HEADS, n_q),
        in_specs=[
            pl.BlockSpec((TQ, V_DIM), lambda b, h, i: (b * n_q + i, h)),
            pl.BlockSpec((None, V_DIM, seq), lambda b, h, i: (b, h, 0)),
            pl.BlockSpec((seq, V_DIM), lambda b, h, i: (b, h)),
            pl.BlockSpec((4, QK_DIM), lambda b, h, i: (0, 0)),
            pl.BlockSpec((1, V_DIM), lambda b, h, i: (0, 0)),
        ],
        out_specs=pl.BlockSpec((TQ, V_DIM), lambda b, h, i: (b * n_q + i, h)),
        out_shape=jax.ShapeDtypeStruct((batch * seq, D_MODEL), F32),
        scratch_shapes=[pltpu.VMEM((2 * TQ, 1), F32), pltpu.VMEM((2 * TQ, 1), F32),
                        pltpu.VMEM((2 * TQ, V_DIM), F32)],
        compiler_params=_cparams(("arbitrary", "arbitrary", "arbitrary")),
        name="flash",
    )(q, kt_b, v_b, lam_p, sub_g)


def _decode_kernel(pt_ref, q_ref, ktn_ref, vn_ref, lp_ref, sg_ref, *rest, lambda_init, n_new):
    del pt_ref
    npg = PAGES_PER_STEP
    k_refs = rest[:npg]
    v_refs = rest[npg:2 * npg]
    o_ref = rest[2 * npg]
    q2_s, m_s, l_s, acc_s = rest[2 * npg + 1:]
    p = pl.program_id(1)
    n_cols = 2 * N_HEADS * n_new

    @pl.when(p == 0)
    def _():
        qq = jnp.concatenate([q_ref[...]] * (2 * N_HEADS), axis=0)
        row = lax.broadcasted_iota(jnp.int32, qq.shape, 0)
        lane = lax.broadcasted_iota(jnp.int32, qq.shape, 1)
        q2_s[...] = jnp.where(lane // QK_DIM == row // n_new, qq, 0.0).astype(BF16)
        m_s[...] = jnp.full_like(m_s, -jnp.inf)
        l_s[...] = jnp.zeros_like(l_s)
        acc_s[...] = jnp.zeros_like(acc_s)

    def update(s, v):
        m_prev = m_s[...]
        m_new = jnp.maximum(m_prev, jnp.max(s, axis=-1, keepdims=True))
        alpha = jnp.exp(m_prev - m_new)
        pr = jnp.exp(s - m_new)
        l_s[...] = alpha * l_s[...] + jnp.sum(pr, axis=-1, keepdims=True)
        acc_s[...] = alpha * acc_s[...] + jnp.dot(pr.astype(BF16), v, preferred_element_type=F32)
        m_s[...] = m_new

    q2 = q2_s[...]
    s = jnp.concatenate(
        [jnp.dot(q2, k_refs[i][...].astype(BF16), preferred_element_type=F32) for i in range(npg)], axis=1)
    v = jnp.concatenate(
        [jnp.concatenate([v_refs[i][pl.ds(h, PAGE_SIZE, stride=N_HEADS), :] for h in range(N_HEADS)], axis=1)
         for i in range(npg)], axis=0).astype(BF16)
    update(s, v)

    @pl.when(p == pl.num_programs(1) - 1)
    def _():
        sn = jnp.dot(q2, ktn_ref[...].astype(BF16), preferred_element_type=F32)
        row = lax.broadcasted_iota(jnp.int32, sn.shape, 0)
        col = lax.broadcasted_iota(jnp.int32, sn.shape, 1)
        sn = jnp.where(col <= row % n_new, sn, -jnp.inf)
        vn = jnp.concatenate([vn_ref[...], jnp.zeros((PAGE_SIZE - n_new, D_MODEL), F32)], axis=0).astype(BF16)
        update(sn, vn)
        lam = _lambda(lp_ref, lambda_init)
        on = acc_s[...] / l_s[...]
        for h in range(N_HEADS):
            blk = on[h * 2 * n_new:(h + 1) * 2 * n_new, h * V_DIM:(h + 1) * V_DIM]
            o_ref[:, h * V_DIM:(h + 1) * V_DIM] = _head_out(blk[:n_new], blk[n_new:], lam, sg_ref[...],
                                                             lambda_init)
    del n_cols


def _decode(pt_flat, q, ktn, v_s, lam_p, sub_g, ck_t, cv_r, layer, n_req, n_new, n_pages, q_row0, lambda_init):
    npg = PAGES_PER_STEP
    n_steps = n_pages // npg
    n_cols = 2 * N_HEADS * n_new
    qb0 = q_row0 // n_new

    def page_spec(i):
        return pl.BlockSpec((None, None, D_MODEL, PAGE_SIZE),
                            lambda b, p, pt: (layer, pt[b * n_pages + p * npg + i], 0, 0))

    return pl.pallas_call(
        functools.partial(_decode_kernel, lambda_init=lambda_init, n_new=n_new),
        grid_spec=pltpu.PrefetchScalarGridSpec(
            num_scalar_prefetch=1,
            grid=(n_req, n_steps),
            in_specs=[
                pl.BlockSpec((n_new, D_MODEL), lambda b, p, pt: (qb0 + b, 0)),
                pl.BlockSpec((None, D_MODEL, PAGE_SIZE), lambda b, p, pt: (b, 0, 0)),
                pl.BlockSpec((n_new, D_MODEL), lambda b, p, pt: (b, 0)),
                pl.BlockSpec((4, QK_DIM), lambda b, p, pt: (0, 0)),
                pl.BlockSpec((1, V_DIM), lambda b, p, pt: (0, 0)),
            ] + [page_spec(i) for i in range(npg)] + [page_spec(i) for i in range(npg)],
            out_specs=pl.BlockSpec((n_new, D_MODEL), lambda b, p, pt: (b, 0)),
            scratch_shapes=[pltpu.VMEM((n_cols, D_MODEL), BF16), pltpu.VMEM((n_cols, 1), F32),
                            pltpu.VMEM((n_cols, 1), F32), pltpu.VMEM((n_cols, D_MODEL), F32)],
        ),
        out_shape=jax.ShapeDtypeStruct((n_req * n_new, D_MODEL), F32),
        compiler_params=_cparams(("arbitrary", "arbitrary")),
        name="decode_attn",
    )(pt_flat, q, ktn, v_s, lam_p, sub_g, *([ck_t] * npg), *([cv_r] * npg))


def _lru_kernel(u_ref, buf_ref, h0_ref, cw_ref, cb_ref, wa_ref, ba_ref, wi_ref, bi_ref, lam_ref,
                y_ref, hl_ref, ct_ref, xbuf, a_s, b_s, hc, *, ts):
    t = pl.program_id(1)

    @pl.when(t == 0)
    def _():
        xbuf[0:8, :] = buf_ref[...]
        hc[...] = jnp.broadcast_to(h0_ref[...], (8, D_RNN))

    xr = u_ref[:, D_RNN:]
    xbuf[8:8 + ts, :] = xr
    cw = cw_ref[...]
    xc = cb_ref[...] + (cw[0:1] * xbuf[5:5 + ts, :] + cw[1:2] * xbuf[6:6 + ts, :]
                        + cw[2:3] * xbuf[7:7 + ts, :] + cw[3:4] * xr)
    z = -lam_ref[...]
    sp = jnp.maximum(z, 0.0) + jnp.log1p(jnp.exp(-jnp.abs(z)))
    row = lax.broadcasted_iota(jnp.int32, (ts, RG_BLOCK), 0) % 8
    for n in range(N_RG_BLOCKS):
        sl = slice(n * RG_BLOCK, (n + 1) * RG_BLOCK)
        xcn = xc[:, sl]
        xb = xcn.astype(BF16)
        r = jax.nn.sigmoid(jnp.dot(xb, wa_ref[n], preferred_element_type=F32) + ba_ref[:, sl])
        ig = jax.nn.sigmoid(jnp.dot(xb, wi_ref[n], preferred_element_type=F32) + bi_ref[:, sl])
        log_a = -RG_C * r * sp[:, sl]
        a = jnp.exp(log_a)
        mult = jnp.sqrt(-jnp.tanh(log_a) * (jnp.exp(2.0 * log_a) + 1.0))
        b = mult * (ig * xcn)
        for s in (1, 2, 4):
            a_sh = pltpu.roll(a, s, 0)
            b_sh = pltpu.roll(b, s, 0)
            keep = row >= s
            b = jnp.where(keep, a * b_sh + b, b)
            a = jnp.where(keep, a * a_sh, a)
        a_s[:, sl] = a
        b_s[:, sl] = b

    def step(i, h):
        r0 = pl.multiple_of(i * 8, 8)
        ht = a_s[pl.ds(r0, 8), :] * h + b_s[pl.ds(r0, 8), :]
        b_s[pl.ds(r0, 8), :] = ht
        return jnp.broadcast_to(ht[7:8, :], (8, D_RNN))

    h = lax.fori_loop(0, ts // 8, step, hc[...])
    hc[...] = h
    xbuf[0:8, :] = xbuf[ts:ts + 8, :]
    y_ref[...] = b_s[...] * jax.nn.gelu(u_ref[:, :D_RNN])

    @pl.when(t == pl.num_programs(1) - 1)
    def _():
        hl_ref[...] = h[0:1]
        ct_ref[...] = xbuf[0:8, :]


def _lru(u, buf0, h0, cw, cb, wa, ba, wi, bi, lam, n_seq, seq, ts, row_blk0):
    nt = seq // ts
    c2 = lambda b, t: (0, 0)
    c3 = lambda b, t: (0, 0, 0)
    return pl.pallas_call(
        functools.partial(_lru_kernel, ts=ts),
        grid=(n_seq, nt),
        in_specs=[
            pl.BlockSpec((ts, 2 * D_RNN), lambda b, t: (row_blk0 + b * nt + t, 0)),
            pl.BlockSpec((None, 8, D_RNN), lambda b, t: (b, 0, 0)),
            pl.BlockSpec((None, 1, D_RNN), lambda b, t: (b, 0, 0)),
            pl.BlockSpec((CONV_W, D_RNN), c2),
            pl.BlockSpec((1, D_RNN), c2),
            pl.BlockSpec((N_RG_BLOCKS, RG_BLOCK, RG_BLOCK), c3),
            pl.BlockSpec((1, D_RNN), c2),
            pl.BlockSpec((N_RG_BLOCKS, RG_BLOCK, RG_BLOCK), c3),
            pl.BlockSpec((1, D_RNN), c2),
            pl.BlockSpec((1, D_RNN), c2),
        ],
        out_specs=[
            pl.BlockSpec((ts, D_RNN), lambda b, t: (b * nt + t, 0)),
            pl.BlockSpec((None, 1, D_RNN), lambda b, t: (b, 0, 0)),
            pl.BlockSpec((None, 8, D_RNN), lambda b, t: (b, 0, 0)),
        ],
        out_shape=[
            jax.ShapeDtypeStruct((n_seq * seq, D_RNN), F32),
            jax.ShapeDtypeStruct((n_seq, 1, D_RNN), F32),
            jax.ShapeDtypeStruct((n_seq, 8, D_RNN), F32),
        ],
        scratch_shapes=[pltpu.VMEM((ts + 8, D_RNN), F32), pltpu.VMEM((ts, D_RNN), F32),
                        pltpu.VMEM((ts, D_RNN), F32), pltpu.VMEM((8, D_RNN), F32)],
        compiler_params=_cparams(("arbitrary", "arbitrary")),
        name="lru",
    )(u, buf0, h0, cw, cb, wa, ba, wi, bi, lam)


def _route(logits, b_group, b_sub):
    t_all = logits.shape[0]
    gl = logits[:, :N_GROUPS] + b_group.astype(F32)
    gp = jax.nn.softmax(gl, axis=-1)
    g_idx = jnp.argmax(gl, axis=-1)
    g_w = jnp.take_along_axis(gp, g_idx[:, None], axis=-1)
    sl = (logits[:, N_GROUPS:N_GROUPS + N_EXPERTS] + b_sub.astype(F32)).reshape(t_all, N_GROUPS, EXPERTS_PER_GROUP)
    sl = jnp.take_along_axis(sl, g_idx[:, None, None], axis=1)[:, 0]
    sp = jax.nn.softmax(sl, axis=-1)
    top_v, top_i = lax.top_k(sp, TOP_K)
    top_v = top_v / jnp.sum(top_v, axis=-1, keepdims=True)
    eid = (g_idx[:, None] * EXPERTS_PER_GROUP + top_i).reshape(t_all * TOP_K).astype(jnp.int32)
    wts = (g_w * top_v).reshape(t_all * TOP_K)
    n = t_all * TOP_K
    onehot = (eid[:, None] == jnp.arange(N_EXPERTS, dtype=jnp.int32)[None, :]).astype(jnp.int32)
    csum = jnp.cumsum(onehot, axis=0)
    counts = csum[-1]
    rank = jnp.take_along_axis(csum, eid[:, None], axis=1)[:, 0] - 1
    pcounts = ((counts + MOE_TM - 1) // MOE_TM) * MOE_TM
    pends = jnp.cumsum(pcounts)
    pstarts = pends - pcounts
    dest = (pstarts[eid] + rank).astype(jnp.int32)
    n_blk = n // MOE_TM + N_EXPERTS
    blk_e = jnp.clip(jnp.searchsorted(pends, jnp.arange(n_blk, dtype=jnp.int32) * MOE_TM, side='right'),
                     0, N_EXPERTS - 1).astype(jnp.int32)
    n_valid = (pends[-1] // MOE_TM).astype(jnp.int32).reshape(1)
    tok = jnp.arange(n, dtype=jnp.int32) // TOP_K
    row_tok = jnp.zeros((n_blk * MOE_TM,), jnp.int32).at[dest].set(tok)
    row_w = jnp.zeros((n_blk * MOE_TM,), F32).at[dest].set(wts)
    return dest, blk_e, n_valid, row_tok, row_w[:, None]


def kernel(x_prompt, x_sample, cache_k, cache_v, state_h, state_conv, page_table, p_prompt, p_sample, mix_norm, ffn_norm, ple_norm, w_qkv, q_norm, k_norm, lambda_q1, lambda_k1, lambda_q2, lambda_k2, sub_norm, w_o, w_in_lru, conv_w, conv_b, w_rg_a, b_rg_a, w_rg_i, b_rg_i, lru_lambda, w_out_lru, w_group, b_group, w_sub, b_sub, w_gate_up, w_down, w_ple_gate, w_ple_proj):
    batch, seq, _ = x_prompt.shape
    n_req, n_new, _ = x_sample.shape
    n_pages = page_table.shape[1]
    past_len = n_pages * PAGE_SIZE
    t_p = batch * seq
    t_s = n_req * n_new
    assert t_s == TM and t_p % TM == 0 and seq % TM == 0 and n_pages % PAGES_PER_STEP == 0
    n_p = t_p // TM
    n_attn = cache_k.shape[0]
    n_pool = cache_k.shape[1]

    h = jnp.concatenate([x_prompt.reshape(t_p, D_MODEL), x_sample.reshape(t_s, D_MODEL)], axis=0)

    pos = jnp.concatenate([jnp.tile(jnp.arange(seq, dtype=jnp.int32), batch),
                           jnp.tile(past_len + jnp.arange(n_new, dtype=jnp.int32), n_req)])
    half = ROPE_DIM // 2
    inv = ROPE_THETA ** (-(jnp.arange(half, dtype=F32) * 2.0 / ROPE_DIM))
    ang = pos.astype(F32)[:, None] * inv[None, :]
    cos_t = jnp.cos(ang).T
    sin_t = jnp.sin(ang).T

    ck_t = cache_k.transpose(0, 1, 3, 4, 5, 2).reshape(n_attn, n_pool, D_MODEL, PAGE_SIZE)
    cv_r = cache_v.reshape(n_attn, n_pool, PAGE_SIZE * N_HEADS, V_DIM)
    pt_flat = page_table.reshape(-1).astype(jnp.int32)
    pp = p_prompt.reshape(DEPTH, t_p, PLE_DIM)
    ps = p_sample.reshape(DEPTH, t_s, PLE_DIM)

    k_p, v_p, k_s, v_s, hl_p, ct_p, hl_s, ct_s = [], [], [], [], [], [], [], []
    y_p = y_s = None
    for i in range(DEPTH):
        j = i // 2
        g_mix = mix_norm[i].reshape(1, D_MODEL).astype(F32)
        if i % 2 == 0:
            lambda_init = 0.8 - 0.6 * math.exp(-0.3 * i)
            wq_t = w_qkv[j, :, :D_MODEL].T.astype(BF16)
            wk_t = w_qkv[j, :, D_MODEL:2 * D_MODEL].T.astype(BF16)
            wv = w_qkv[j, :, 2 * D_MODEL:].astype(BF16)
            qg_b = jnp.broadcast_to(q_norm[j].astype(F32)[:, None], (QK_DIM, TM))
            kg_b = jnp.broadcast_to(k_norm[j].astype(F32)[:, None], (QK_DIM, TM))
            q, kt_p, kt_s, kt_b, vv_p, vv_s, v_b = _qkv(h, g_mix, wq_t, wk_t, wv, qg_b, kg_b, cos_t, sin_t,
                                                       n_p, batch, seq)
            lam_p = jnp.stack([lambda_q1[j], lambda_k1[j], lambda_q2[j], lambda_k2[j]]).astype(F32)
            sub_g = sub_norm[j].reshape(1, V_DIM).astype(F32)
            o_p = _flash(q, kt_b, v_b, lam_p, sub_g, batch, seq, lambda_init)
            ktn = kt_s.reshape(D_MODEL, n_req, n_new).transpose(1, 0, 2)
            ktn = jnp.pad(ktn, ((0, 0), (0, 0), (0, PAGE_SIZE - n_new)))
            o_s = _decode(pt_flat, q, ktn, vv_s, lam_p, sub_g, ck_t, cv_r, j, n_req, n_new, n_pages, t_p,
                          lambda_init)
            h = _resid_matmul(h, o_p, o_s, w_o[j].astype(BF16), n_p)
            k_p.append(kt_p.reshape(batch, N_HEADS, 2, QK_DIM, seq).transpose(0, 4, 1, 2, 3))
            v_p.append(vv_p.reshape(batch, seq, N_HEADS, V_DIM))
            k_s.append(kt_s.T.reshape(n_req, n_new, N_HEADS, 2, QK_DIM))
            v_s.append(vv_s.reshape(n_req, n_new, N_HEADS, V_DIM))
        else:
            u = _norm_matmul(h, g_mix, w_in_lru[j].astype(BF16))
            cw = conv_w[j].astype(F32)
            cb = conv_b[j].reshape(1, D_RNN).astype(F32)
            wa = w_rg_a[j].astype(BF16)
            wi = w_rg_i[j].astype(BF16)
            ba = b_rg_a[j].reshape(1, D_RNN).astype(F32)
            bi = b_rg_i[j].reshape(1, D_RNN).astype(F32)
            lam = lru_lambda[j].reshape(1, D_RNN).astype(F32)
            zb = jnp.zeros((batch, 8, D_RNN), F32)
            zh = jnp.zeros((batch, 1, D_RNN), F32)
            yl_p, hlp, ctp = _lru(u, zb, zh, cw, cb, wa, ba, wi, bi, lam, batch, seq, LRU_TS, 0)
            buf_s = jnp.pad(state_conv[j].astype(F32), ((0, 0), (8 - (CONV_W - 1), 0), (0, 0)))
            h0_s = state_h[j].astype(F32)[:, None, :]
            yl_s, hls, cts = _lru(u, buf_s, h0_s, cw, cb, wa, ba, wi, bi, lam, n_req, n_new, n_new,
                                  t_p // n_new)
            h = _resid_matmul(h, yl_p, yl_s, w_out_lru[j].astype(BF16), n_p)
            hl_p.append(hlp[:, 0])
            ct_p.append(ctp[:, 8 - (CONV_W - 1):])
            hl_s.append(hls[:, 0])
            ct_s.append(cts[:, 8 - (CONV_W - 1):])

        w_r = jnp.concatenate([w_group[i], w_sub[i]], axis=1).astype(F32)
        w_r = jnp.pad(w_r, ((0, 0), (0, 128 - w_r.shape[1])))
        w_hi = w_r.astype(BF16)
        w_lo = (w_r - w_hi.astype(F32)).astype(BF16)
        hn_b, logits = _router(h, ffn_norm[i].reshape(1, D_MODEL).astype(F32), w_hi, w_lo)
        dest, blk_e, n_valid, row_tok, row_w = _route(logits, b_group[i], b_sub[i])
        xs = jnp.take(hn_b, row_tok, axis=0)
        ys = _moe_ffn(blk_e, n_valid, xs, row_w, w_gate_up[i].astype(BF16), w_down[i].astype(BF16))
        ma = jnp.take(ys, dest[0::2], axis=0)
        mb = jnp.take(ys, dest[1::2], axis=0)

        last = i == DEPTH - 1
        out = _ple(h, ma, mb, ple_norm[i].reshape(1, D_MODEL).astype(F32), w_ple_gate[i].astype(BF16),
                   pp, ps, w_ple_proj[i].astype(BF16), i, n_p, last)
        if last:
            y_p, y_s = out
        else:
            h = out

    return (y_p.reshape(batch, seq, D_MODEL), y_s.reshape(n_req, n_new, D_MODEL),
            jnp.stack(k_p), jnp.stack(v_p), jnp.stack(hl_p), jnp.stack(ct_p),
            jnp.stack(k_s), jnp.stack(v_s), jnp.stack(hl_s), jnp.stack(ct_s))
```

```python
import functools
import math

import jax
import jax.numpy as jnp
import numpy as np
from jax import lax
from jax.experimental import pallas as pl
from jax.experimental.pallas import tpu as pltpu

F32 = jnp.float32
BF16 = jnp.bfloat16

D_MODEL = 1024
DEPTH = 4
PAGE_SIZE = 128
N_HEADS = 8
QK_DIM = 64
V_DIM = 128
ROPE_DIM = 16
ROPE_THETA = 500000.0
D_RNN = 1280
RG_BLOCK = 128
N_RG_BLOCKS = 10
CONV_W = 4
RG_C = 8.0
N_GROUPS = 4
EXPERTS_PER_GROUP = 8
N_EXPERTS = 32
TOP_K = 2
D_EXPERT = 512
PLE_DIM = 256
EPS = 1e-6

TM = 256
TQ = 256
PAGES_PER_STEP = 4
MOE_TM = 256
LRU_TS = 256
VMEM_LIMIT = 56 * 1024 * 1024

_NT = (((1,), (1,)), ((), ()))


def _cparams(sem):
    return pltpu.CompilerParams(dimension_semantics=sem, vmem_limit_bytes=VMEM_LIMIT)


def _rms(x, g):
    return x * lax.rsqrt(jnp.mean(x * x, axis=-1, keepdims=True) + EPS) * g


def _tile(i, n_p):
    return jnp.where(i == 0, n_p, i - 1)


def _ptile(i):
    return jnp.maximum(i - 1, 0)


def _qkv_kernel(h_ref, g_ref, wq_ref, wk_ref, wv_ref, qg_ref, kg_ref, cos_ref, sin_ref,
                qtb_ref, qs_ref, ktp_ref, kts_ref, kb_ref, vp_ref, vs_ref, vtb_ref, qt_s, kt_s):
    i = pl.program_id(0)
    hn = _rms(h_ref[...], g_ref[...]).astype(BF16)
    cos = cos_ref[...]
    sin = sin_ref[...]

    def normrope(w_ref, gain_ref, dst, scale):
        xt = lax.dot_general(w_ref[...], hn, _NT, preferred_element_type=F32)
        gain = gain_ref[...]
        for g in range(2 * N_HEADS):
            xg = xt[g * QK_DIM:(g + 1) * QK_DIM, :]
            ss = jnp.sum(xg * xg, axis=0, keepdims=True)
            yg = xg * lax.rsqrt(ss * (1.0 / QK_DIM) + EPS) * gain
            y1 = yg[0:8]
            y2 = yg[8:16]
            dst[g * QK_DIM:g * QK_DIM + 8, :] = (y1 * cos - y2 * sin) * scale
            dst[g * QK_DIM + 8:g * QK_DIM + 16, :] = (y2 * cos + y1 * sin) * scale
            dst[g * QK_DIM + 16:(g + 1) * QK_DIM, :] = yg[16:] * scale

    normrope(wq_ref, qg_ref, qt_s, QK_DIM ** -0.5)
    normrope(wk_ref, kg_ref, kt_s, 1.0)
    v = jnp.dot(hn, wv_ref[...], preferred_element_type=F32)

    @pl.when(i == 0)
    def _():
        qs_ref[...] = qt_s[...].T
        kts_ref[...] = kt_s[...]
        vs_ref[...] = v

    @pl.when(i > 0)
    def _():
        kt = kt_s[...]
        qtb_ref[...] = qt_s[...].astype(BF16)
        ktp_ref[...] = kt
        kb_ref[...] = kt.T.astype(BF16)
        vp_ref[...] = v
        vtb_ref[...] = v.T.astype(BF16)


def _qkv(h, g, wq_t, wk_t, wv, qg_b, kg_b, cos_t, sin_t, n_p, batch, seq):
    t_all = h.shape[0]
    n_sb = seq // TM
    const = lambda i: (0, 0)
    tile = lambda i: (_tile(i, n_p), 0)
    ktile = lambda i: (_ptile(i) // n_sb, 0, _ptile(i) % n_sb)
    return pl.pallas_call(
        _qkv_kernel,
        grid=(n_p + 1,),
        in_specs=[
            pl.BlockSpec((TM, D_MODEL), tile),
            pl.BlockSpec((1, D_MODEL), const),
            pl.BlockSpec((D_MODEL, D_MODEL), const),
            pl.BlockSpec((D_MODEL, D_MODEL), const),
            pl.BlockSpec((D_MODEL, D_MODEL), const),
            pl.BlockSpec((QK_DIM, TM), const),
            pl.BlockSpec((QK_DIM, TM), const),
            pl.BlockSpec((8, TM), lambda i: (0, _tile(i, n_p))),
            pl.BlockSpec((8, TM), lambda i: (0, _tile(i, n_p))),
        ],
        out_specs=[
            pl.BlockSpec((None, D_MODEL, TM), ktile),
            pl.BlockSpec((TM, D_MODEL), const),
            pl.BlockSpec((None, D_MODEL, TM), ktile),
            pl.BlockSpec((D_MODEL, TM), const),
            pl.BlockSpec((TM, D_MODEL), lambda i: (_ptile(i), 0)),
            pl.BlockSpec((TM, D_MODEL), lambda i: (_ptile(i), 0)),
            pl.BlockSpec((TM, D_MODEL), const),
            pl.BlockSpec((None, D_MODEL, TM), ktile),
        ],
        out_shape=[
            jax.ShapeDtypeStruct((batch, D_MODEL, seq), BF16),
            jax.ShapeDtypeStruct((TM, D_MODEL), F32),
            jax.ShapeDtypeStruct((batch, D_MODEL, seq), F32),
            jax.ShapeDtypeStruct((D_MODEL, TM), F32),
            jax.ShapeDtypeStruct((n_p * TM, D_MODEL), BF16),
            jax.ShapeDtypeStruct((n_p * TM, D_MODEL), F32),
            jax.ShapeDtypeStruct((TM, D_MODEL), F32),
            jax.ShapeDtypeStruct((batch, D_MODEL, seq), BF16),
        ],
        scratch_shapes=[pltpu.VMEM((D_MODEL, TM), F32), pltpu.VMEM((D_MODEL, TM), F32)],
        compiler_params=_cparams(("arbitrary",)),
        name="qkv",
    )(h, g, wq_t, wk_t, wv, qg_b, kg_b, cos_t, sin_t)


def _resid_matmul_kernel(h_ref, xp_ref, xs_ref, w_ref, o_ref):
    i = pl.program_id(0)
    x = jnp.where(i == 0, xs_ref[...], xp_ref[...]).astype(BF16)
    o_ref[...] = h_ref[...] + jnp.dot(x, w_ref[...], preferred_element_type=F32)


def _resid_matmul(h, x_p, x_s, w, n_p):
    t_all = h.shape[0]
    k = w.shape[0]
    tile = lambda i: (_tile(i, n_p), 0)
    return pl.pallas_call(
        _resid_matmul_kernel,
        grid=(n_p + 1,),
        in_specs=[
            pl.BlockSpec((TM, D_MODEL), tile),
            pl.BlockSpec((TM, k), lambda i: (_ptile(i), 0)),
            pl.BlockSpec((TM, k), lambda i: (0, 0)),
            pl.BlockSpec((k, D_MODEL), lambda i: (0, 0)),
        ],
        out_specs=pl.BlockSpec((TM, D_MODEL), tile),
        out_shape=jax.ShapeDtypeStruct((t_all, D_MODEL), F32),
        compiler_params=_cparams(("arbitrary",)),
        name="resid_matmul",
    )(h, x_p, x_s, w)


def _norm_matmul_kernel(h_ref, g_ref, w_ref, o_ref):
    hn = _rms(h_ref[...], g_ref[...]).astype(BF16)
    o_ref[...] = jnp.dot(hn, w_ref[...], preferred_element_type=F32)


def _norm_matmul(h, g, w):
    t_all = h.shape[0]
    n = w.shape[1]
    return pl.pallas_call(
        _norm_matmul_kernel,
        grid=(t_all // TM,),
        in_specs=[
            pl.BlockSpec((TM, D_MODEL), lambda i: (i, 0)),
            pl.BlockSpec((1, D_MODEL), lambda i: (0, 0)),
            pl.BlockSpec((D_MODEL, n), lambda i: (0, 0)),
        ],
        out_specs=pl.BlockSpec((TM, n), lambda i: (i, 0)),
        out_shape=jax.ShapeDtypeStruct((t_all, n), F32),
        compiler_params=_cparams(("arbitrary",)),
        name="norm_matmul",
    )(h, g, w)


_R_E1, _R_E2, _R_W1, _R_W2, _R_RANK1, _R_RANK2 = range(6)
_R_LANES = 128
_EXPERT_LANE0 = N_GROUPS


def _router_kernel(h_ref, g_ref, whi_ref, wlo_ref, bias_ref, hn_ref, info_ref, cnt_ref, cnt_s):
    i = pl.program_id(0)

    @pl.when(i == 0)
    def _():
        cnt_s[...] = jnp.zeros_like(cnt_s)

    hn = _rms(h_ref[...], g_ref[...])
    hn_ref[...] = hn
    hi = hn.astype(BF16)
    lo = (hn - hi.astype(F32)).astype(BF16)
    whi = whi_ref[...]
    lg = jnp.dot(hi, whi, preferred_element_type=F32)
    lg += jnp.dot(lo, whi, preferred_element_type=F32)
    lg += jnp.dot(hi, wlo_ref[...], preferred_element_type=F32)
    lg += bias_ref[...]

    lane = lax.broadcasted_iota(jnp.int32, lg.shape, 1).astype(F32)
    far = float(_R_LANES)
    rmax = lambda x: jnp.max(x, axis=-1, keepdims=True)
    rmin = lambda x: jnp.min(x, axis=-1, keepdims=True)
    rsum = lambda x: jnp.sum(x, axis=-1, keepdims=True)

    gl = jnp.where(lane < N_GROUPS, lg, -jnp.inf)
    gmax = rmax(gl)
    ge = jnp.exp(gl - gmax)
    gp = ge / rsum(ge)
    g_idx = rmin(jnp.where(gl == gmax, lane, far))
    g_w = rsum(jnp.where(lane == g_idx, gp, 0.0))

    lane0 = _EXPERT_LANE0 + g_idx * EXPERTS_PER_GROUP
    smask = (lane >= lane0) & (lane < lane0 + EXPERTS_PER_GROUP)
    sl = jnp.where(smask, lg, -jnp.inf)
    se = jnp.exp(sl - rmax(sl))
    sp = jnp.where(smask, se / rsum(se), -1.0)
    v1 = rmax(sp)
    i1 = rmin(jnp.where(sp == v1, lane, far))
    sp2 = jnp.where(lane == i1, -1.0, sp)
    v2 = rmax(sp2)
    i2 = rmin(jnp.where(sp2 == v2, lane, far))
    den = v1 + v2
    w1 = g_w * (v1 / den)
    w2 = g_w * (v2 / den)

    oh1 = jnp.where(lane == i1, 1.0, 0.0)
    oh2 = jnp.where(lane == i2, 1.0, 0.0)
    r = lax.broadcasted_iota(jnp.int32, (TM, TM), 0)
    c = lax.broadcasted_iota(jnp.int32, (TM, TM), 1)
    tri = jnp.where(c < r, 1.0, 0.0).astype(BF16)
    pre1 = jnp.dot(tri, oh1.astype(BF16), preferred_element_type=F32)
    pre2 = jnp.dot(tri, oh2.astype(BF16), preferred_element_type=F32)
    c1 = jnp.sum(oh1, axis=0, keepdims=True)
    c2 = jnp.sum(oh2, axis=0, keepdims=True)
    base = cnt_s[...]
    rank1 = rsum(oh1 * (pre1 + base))
    rank2 = rsum(oh2 * (pre2 + (base + c1)))
    total = base + (c1 + c2)
    cnt_s[...] = total
    cnt_ref[...] = total

    info = jnp.zeros_like(lg)
    for k, val in ((_R_E1, i1 - _EXPERT_LANE0), (_R_E2, i2 - _EXPERT_LANE0), (_R_W1, w1), (_R_W2, w2),
                   (_R_RANK1, rank1), (_R_RANK2, rank2)):
        info = jnp.where(lane == float(k), val, info)
    info_ref[...] = info


def _router(h, g, w_hi, w_lo, bias):
    t_all = h.shape[0]
    const = lambda i: (0, 0)
    return pl.pallas_call(
        _router_kernel,
        grid=(t_all // TM,),
        in_specs=[
            pl.BlockSpec((TM, D_MODEL), lambda i: (i, 0)),
            pl.BlockSpec((1, D_MODEL), const),
            pl.BlockSpec((D_MODEL, _R_LANES), const),
            pl.BlockSpec((D_MODEL, _R_LANES), const),
            pl.BlockSpec((1, _R_LANES), const),
        ],
        out_specs=[
            pl.BlockSpec((TM, D_MODEL), lambda i: (i, 0)),
            pl.BlockSpec((TM, _R_LANES), lambda i: (i, 0)),
            pl.BlockSpec((1, _R_LANES), const),
        ],
        out_shape=[
            jax.ShapeDtypeStruct((t_all, D_MODEL), F32),
            jax.ShapeDtypeStruct((t_all, _R_LANES), F32),
            jax.ShapeDtypeStruct((1, _R_LANES), F32),
        ],
        scratch_shapes=[pltpu.VMEM((1, _R_LANES), F32)],
        compiler_params=_cparams(("arbitrary",)),
        name="router",
    )(h, g, w_hi, w_lo, bias)


def _moe_ffn_kernel(be_ref, nv_ref, x_ref, wgu_ref, wd_ref, o_ref, wgu_s, wd_s):
    j = pl.program_id(0)
    valid = j < nv_ref[0]
    new_expert = (j == 0) | (be_ref[j] != be_ref[jnp.maximum(j - 1, 0)])

    @pl.when(valid & new_expert)
    def _():
        wgu_s[...] = wgu_ref[...].astype(BF16)
        wd_s[...] = wd_ref[...].astype(BF16)

    @pl.when(valid)
    def _():
        gu = jnp.dot(x_ref[...].astype(BF16), wgu_s[...], preferred_element_type=F32)
        act = jax.nn.silu(gu[:, :D_EXPERT]) * gu[:, D_EXPERT:]
        o_ref[...] = jnp.dot(act.astype(BF16), wd_s[...], preferred_element_type=F32)

    @pl.when(jnp.logical_not(valid))
    def _():
        o_ref[...] = jnp.zeros_like(o_ref)


def _moe_ffn(blk_e, n_valid, xs, wgu, wd, layer):
    n_rows = xs.shape[0]
    n_blk = n_rows // MOE_TM
    return pl.pallas_call(
        _moe_ffn_kernel,
        grid_spec=pltpu.PrefetchScalarGridSpec(
            num_scalar_prefetch=2,
            grid=(n_blk,),
            in_specs=[
                pl.BlockSpec((MOE_TM, D_MODEL), lambda j, be, nv: (j, 0)),
                pl.BlockSpec((None, None, D_MODEL, 2 * D_EXPERT), lambda j, be, nv: (layer, be[j], 0, 0)),
                pl.BlockSpec((None, None, D_EXPERT, D_MODEL), lambda j, be, nv: (layer, be[j], 0, 0)),
            ],
            out_specs=pl.BlockSpec((MOE_TM, D_MODEL), lambda j, be, nv: (j, 0)),
            scratch_shapes=[pltpu.VMEM((D_MODEL, 2 * D_EXPERT), BF16), pltpu.VMEM((D_EXPERT, D_MODEL), BF16)],
        ),
        out_shape=jax.ShapeDtypeStruct((n_rows, D_MODEL), F32),
        compiler_params=_cparams(("arbitrary",)),
        name="moe_ffn",
    )(blk_e, n_valid, xs, wgu, wd)


def _ple_kernel(h_ref, ma_ref, mb_ref, info_ref, g_ref, wg_ref, pp_ref, ps_ref, wp_ref, *o_refs, split):
    i = pl.program_id(0)
    info = info_ref[...]
    w1 = info[:, _R_W1:_R_W1 + 1]
    w2 = info[:, _R_W2:_R_W2 + 1]
    x = h_ref[...] + (w1 * ma_ref[...] + w2 * mb_ref[...])
    hn = _rms(x, g_ref[...]).astype(BF16)
    gate = jax.nn.sigmoid(jnp.dot(hn, wg_ref[...], preferred_element_type=F32))
    p = jnp.where(i == 0, ps_ref[...], pp_ref[...]).astype(BF16)
    out = x + gate * jnp.dot(p, wp_ref[...], preferred_element_type=F32)
    if split:
        yp_ref, ys_ref = o_refs

        @pl.when(i == 0)
        def _():
            ys_ref[...] = out

        @pl.when(i > 0)
        def _():
            yp_ref[...] = out
    else:
        o_refs[0][...] = out


def _ple(h, ma, mb, info, g, wg, p_p, p_s, wp, layer, n_p, split):
    t_all = h.shape[0]
    tile = lambda i: (_tile(i, n_p), 0)
    const = lambda i: (0, 0)
    if split:
        out_specs = [pl.BlockSpec((TM, D_MODEL), lambda i: (_ptile(i), 0)),
                     pl.BlockSpec((TM, D_MODEL), const)]
        out_shape = [jax.ShapeDtypeStruct((n_p * TM, D_MODEL), F32),
                     jax.ShapeDtypeStruct((TM, D_MODEL), F32)]
    else:
        out_specs = pl.BlockSpec((TM, D_MODEL), tile)
        out_shape = jax.ShapeDtypeStruct((t_all, D_MODEL), F32)
    return pl.pallas_call(
        functools.partial(_ple_kernel, split=split),
        grid=(n_p + 1,),
        in_specs=[
            pl.BlockSpec((TM, D_MODEL), tile),
            pl.BlockSpec((TM, D_MODEL), tile),
            pl.BlockSpec((TM, D_MODEL), tile),
            pl.BlockSpec((TM, _R_LANES), tile),
            pl.BlockSpec((1, D_MODEL), const),
            pl.BlockSpec((D_MODEL, D_MODEL), const),
            pl.BlockSpec((None, TM, PLE_DIM), lambda i: (layer, _ptile(i), 0)),
            pl.BlockSpec((None, TM, PLE_DIM), lambda i: (layer, 0, 0)),
            pl.BlockSpec((PLE_DIM, D_MODEL), const),
        ],
        out_specs=out_specs,
        out_shape=out_shape,
        compiler_params=_cparams(("arbitrary",)),
        name="ple",
    )(h, ma, mb, info, g, wg, p_p, p_s, wp)


def _lambda(lp_ref, lambda_init):
    lp = lp_ref[...]
    s1 = jnp.sum(lp[0:1] * lp[1:2], axis=-1, keepdims=True)
    s2 = jnp.sum(lp[2:3] * lp[3:4], axis=-1, keepdims=True)
    return jnp.exp(s1) - jnp.exp(s2) + lambda_init


def _head_out(o0, o1, lam, sg, lambda_init):
    o = o0 - lam * o1
    return _rms(o, sg) * (1.0 - lambda_init)


def _flash_kernel(qt_ref, k_ref, vt_ref, lp_ref, sg_ref, o_ref, m_s, l_s, acc_s, *, lambda_init):
    qi = pl.program_id(2)
    qt = qt_ref[...]
    zero = jnp.zeros((QK_DIM, TQ), qt.dtype)
    q2t = jnp.concatenate([jnp.concatenate([qt[:QK_DIM], zero], axis=0),
                           jnp.concatenate([zero, qt[QK_DIM:]], axis=0)], axis=1)
    acc_s[...] = jnp.zeros_like(acc_s)

    def attend(j, carry, masked):
        m_prev, l_prev = carry
        c0 = pl.multiple_of(j * TQ, TQ)
        st = jnp.dot(k_ref[pl.ds(c0, TQ), :], q2t, preferred_element_type=F32)
        if masked:
            key = lax.broadcasted_iota(jnp.int32, st.shape, 0)
            col = lax.broadcasted_iota(jnp.int32, st.shape, 1)
            st = jnp.where(key <= jnp.where(col >= TQ, col - TQ, col), st, -jnp.inf)
        m_new = jnp.maximum(m_prev, jnp.max(st, axis=0, keepdims=True))
        alpha = jnp.exp(m_prev - m_new)
        p = jnp.exp(st - m_new)
        l_new = alpha * l_prev + jnp.sum(p, axis=0, keepdims=True)
        pv = jnp.dot(vt_ref[:, pl.ds(c0, TQ)], p.astype(BF16), preferred_element_type=F32)
        acc_s[...] = alpha * acc_s[...] + pv
        return m_new, l_new

    def pair(jj, carry):
        carry = attend(2 * jj, carry, False)
        return attend(2 * jj + 1, carry, False)

    init = (jnp.full((1, 2 * TQ), -jnp.inf, F32), jnp.zeros((1, 2 * TQ), F32))
    m, l = lax.fori_loop(0, qi // 2, pair, init)
    m_s[...] = m
    l_s[...] = l

    @pl.when(qi % 2 == 1)
    def _():
        m1, l1 = attend(qi - 1, (m_s[...], l_s[...]), False)
        m_s[...] = m1
        l_s[...] = l1

    _, l = attend(qi, (m_s[...], l_s[...]), True)

    on = acc_s[...] * (1.0 / l)
    lam = _lambda(lp_ref, lambda_init)
    ot = on[:, :TQ] - lam * on[:, TQ:]
    y = ot * lax.rsqrt(jnp.mean(ot * ot, axis=0, keepdims=True) + EPS) * sg_ref[...]
    o_ref[...] = (y * (1.0 - lambda_init)).T


def _flash(qt_b, k_b, vt_b, lam_p, sub_g_b, batch, seq, lambda_init):
    n_q = seq // TQ
    return pl.pallas_call(
        functools.partial(_flash_kernel, lambda_init=lambda_init),
        grid=(batch, N_HEADS, n_q),
        in_specs=[
            pl.BlockSpec((None, V_DIM, TQ), lambda b, h, i: (b, h, i)),
            pl.BlockSpec((seq, V_DIM), lambda b, h, i: (b, h)),
            pl.BlockSpec((None, V_DIM, seq), lambda b, h, i: (b, h, 0)),
            pl.BlockSpec((4, QK_DIM), lambda b, h, i: (0, 0)),
            pl.BlockSpec((V_DIM, TQ), lambda b, h, i: (0, 0)),
        ],
        out_specs=pl.BlockSpec((TQ, V_DIM), lambda b, h, i: (b * n_q + i, h)),
        out_shape=jax.ShapeDtypeStruct((batch * seq, D_MODEL), F32),
        scratch_shapes=[pltpu.VMEM((1, 2 * TQ), F32), pltpu.VMEM((1, 2 * TQ), F32),
                        pltpu.VMEM((V_DIM, 2 * TQ), F32)],
        compiler_params=_cparams(("arbitrary", "arbitrary", "arbitrary")),
        name="flash",
    )(qt_b, k_b, vt_b, lam_p, sub_g_b)


def _decode_kernel(pt_ref, q_ref, ktn_ref, vn_ref, lp_ref, sg_ref, *rest, lambda_init, n_new):
    del pt_ref
    npg = PAGES_PER_STEP
    k_refs = rest[:npg]
    v_refs = rest[npg:2 * npg]
    o_ref = rest[2 * npg]
    q2_s, m_s, l_s, acc_s = rest[2 * npg + 1:]
    p = pl.program_id(1)

    @pl.when(p == 0)
    def _():
        qq = jnp.concatenate([q_ref[...]] * (2 * N_HEADS), axis=0)
        row = lax.broadcasted_iota(jnp.int32, qq.shape, 0)
        lane = lax.broadcasted_iota(jnp.int32, qq.shape, 1)
        q2_s[...] = jnp.where(lane // QK_DIM == row // n_new, qq, 0.0).astype(BF16)
        m_s[...] = jnp.full_like(m_s, -jnp.inf)
        l_s[...] = jnp.zeros_like(l_s)
        acc_s[...] = jnp.zeros_like(acc_s)

    def update(s, v):
        m_prev = m_s[...]
        m_new = jnp.maximum(m_prev, jnp.max(s, axis=-1, keepdims=True))
        alpha = jnp.exp(m_prev - m_new)
        pr = jnp.exp(s - m_new)
        l_s[...] = alpha * l_s[...] + jnp.sum(pr, axis=-1, keepdims=True)
        acc_s[...] = alpha * acc_s[...] + jnp.dot(pr.astype(BF16), v, preferred_element_type=F32)
        m_s[...] = m_new

    q2 = q2_s[...]
    s = jnp.concatenate(
        [jnp.dot(q2, k_refs[i][...].astype(BF16), preferred_element_type=F32) for i in range(npg)], axis=1)
    v = jnp.concatenate(
        [jnp.concatenate([v_refs[i][pl.ds(h, PAGE_SIZE, stride=N_HEADS), :] for h in range(N_HEADS)], axis=1)
         for i in range(npg)], axis=0).astype(BF16)
    update(s, v)

    @pl.when(p == pl.num_programs(1) - 1)
    def _():
        sn = jnp.dot(q2, ktn_ref[...].astype(BF16), preferred_element_type=F32)
        row = lax.broadcasted_iota(jnp.int32, sn.shape, 0)
        col = lax.broadcasted_iota(jnp.int32, sn.shape, 1)
        sn = jnp.where(col <= row % n_new, sn, -jnp.inf)
        vn = jnp.concatenate([vn_ref[...], jnp.zeros((PAGE_SIZE - n_new, D_MODEL), F32)], axis=0).astype(BF16)
        update(sn, vn)
        lam = _lambda(lp_ref, lambda_init)
        on = acc_s[...] / l_s[...]
        for h in range(N_HEADS):
            blk = on[h * 2 * n_new:(h + 1) * 2 * n_new, h * V_DIM:(h + 1) * V_DIM]
            o_ref[:, h * V_DIM:(h + 1) * V_DIM] = _head_out(blk[:n_new], blk[n_new:], lam, sg_ref[...],
                                                             lambda_init)


def _decode(pt_flat, q, ktn, v_s, lam_p, sub_g, ck_t, cv_r, layer, n_req, n_new, n_pages, lambda_init):
    npg = PAGES_PER_STEP
    n_steps = n_pages // npg
    n_cols = 2 * N_HEADS * n_new

    def page_spec(i):
        return pl.BlockSpec((None, None, D_MODEL, PAGE_SIZE),
                            lambda b, p, pt: (layer, pt[b * n_pages + p * npg + i], 0, 0))

    return pl.pallas_call(
        functools.partial(_decode_kernel, lambda_init=lambda_init, n_new=n_new),
        grid_spec=pltpu.PrefetchScalarGridSpec(
            num_scalar_prefetch=1,
            grid=(n_req, n_steps),
            in_specs=[
                pl.BlockSpec((n_new, D_MODEL), lambda b, p, pt: (b, 0)),
                pl.BlockSpec((None, D_MODEL, PAGE_SIZE), lambda b, p, pt: (b, 0, 0)),
                pl.BlockSpec((n_new, D_MODEL), lambda b, p, pt: (b, 0)),
                pl.BlockSpec((4, QK_DIM), lambda b, p, pt: (0, 0)),
                pl.BlockSpec((1, V_DIM), lambda b, p, pt: (0, 0)),
            ] + [page_spec(i) for i in range(npg)] + [page_spec(i) for i in range(npg)],
            out_specs=pl.BlockSpec((n_new, D_MODEL), lambda b, p, pt: (b, 0)),
            scratch_shapes=[pltpu.VMEM((n_cols, D_MODEL), BF16), pltpu.VMEM((n_cols, 1), F32),
                            pltpu.VMEM((n_cols, 1), F32), pltpu.VMEM((n_cols, D_MODEL), F32)],
        ),
        out_shape=jax.ShapeDtypeStruct((n_req * n_new, D_MODEL), F32),
        compiler_params=_cparams(("arbitrary", "arbitrary")),
        name="decode_attn",
    )(pt_flat, q, ktn, v_s, lam_p, sub_g, *([ck_t] * npg), *([cv_r] * npg))


def _lru_kernel(u_ref, buf_ref, h0_ref, cw_ref, cb_ref, wa_ref, ba_ref, wi_ref, bi_ref, lam_ref,
                y_ref, hl_ref, ct_ref, xbuf, a_s, b_s, hc, *, ts):
    t = pl.program_id(1)

    @pl.when(t == 0)
    def _():
        xbuf[0:8, :] = buf_ref[...]
        hc[...] = jnp.broadcast_to(h0_ref[...], (8, D_RNN))

    xr = u_ref[:, D_RNN:]
    xbuf[8:8 + ts, :] = xr
    cw = cw_ref[...]
    xc = cb_ref[...] + (cw[0:1] * xbuf[5:5 + ts, :] + cw[1:2] * xbuf[6:6 + ts, :]
                        + cw[2:3] * xbuf[7:7 + ts, :] + cw[3:4] * xr)
    z = -lam_ref[...]
    sp = jnp.maximum(z, 0.0) + jnp.log1p(jnp.exp(-jnp.abs(z)))
    row = lax.broadcasted_iota(jnp.int32, (ts, RG_BLOCK), 0) % 8
    for n in range(N_RG_BLOCKS):
        sl = slice(n * RG_BLOCK, (n + 1) * RG_BLOCK)
        xcn = xc[:, sl]
        xb = xcn.astype(BF16)
        r = jax.nn.sigmoid(jnp.dot(xb, wa_ref[n], preferred_element_type=F32) + ba_ref[:, sl])
        ig = jax.nn.sigmoid(jnp.dot(xb, wi_ref[n], preferred_element_type=F32) + bi_ref[:, sl])
        log_a = -RG_C * r * sp[:, sl]
        a = jnp.exp(log_a)
        mult = jnp.sqrt(-jnp.tanh(log_a) * (jnp.exp(2.0 * log_a) + 1.0))
        b = mult * (ig * xcn)
        for s in (1, 2, 4):
            a_sh = pltpu.roll(a, s, 0)
            b_sh = pltpu.roll(b, s, 0)
            keep = row >= s
            b = jnp.where(keep, a * b_sh + b, b)
            a = jnp.where(keep, a * a_sh, a)
        a_s[:, sl] = a
        b_s[:, sl] = b

    def step(i, h):
        r0 = pl.multiple_of(i * 8, 8)
        ht = a_s[pl.ds(r0, 8), :] * h + b_s[pl.ds(r0, 8), :]
        b_s[pl.ds(r0, 8), :] = ht
        return jnp.broadcast_to(ht[7:8, :], (8, D_RNN))

    h = lax.fori_loop(0, ts // 8, step, hc[...])
    hc[...] = h
    xbuf[0:8, :] = xbuf[ts:ts + 8, :]
    y_ref[...] = b_s[...] * jax.nn.gelu(u_ref[:, :D_RNN])

    @pl.when(t == pl.num_programs(1) - 1)
    def _():
        hl_ref[...] = h[0:1]
        ct_ref[...] = xbuf[0:8, :]


def _lru(u, buf0, h0, cw, cb, wa, ba, wi, bi, lam, n_seq, seq, ts, row_blk0):
    nt = seq // ts
    c2 = lambda b, t: (0, 0)
    c3 = lambda b, t: (0, 0, 0)
    return pl.pallas_call(
        functools.partial(_lru_kernel, ts=ts),
        grid=(n_seq, nt),
        in_specs=[
            pl.BlockSpec((ts, 2 * D_RNN), lambda b, t: (row_blk0 + b * nt + t, 0)),
            pl.BlockSpec((None, 8, D_RNN), lambda b, t: (b, 0, 0)),
            pl.BlockSpec((None, 1, D_RNN), lambda b, t: (b, 0, 0)),
            pl.BlockSpec((CONV_W, D_RNN), c2),
            pl.BlockSpec((1, D_RNN), c2),
            pl.BlockSpec((N_RG_BLOCKS, RG_BLOCK, RG_BLOCK), c3),
            pl.BlockSpec((1, D_RNN), c2),
            pl.BlockSpec((N_RG_BLOCKS, RG_BLOCK, RG_BLOCK), c3),
            pl.BlockSpec((1, D_RNN), c2),
            pl.BlockSpec((1, D_RNN), c2),
        ],
        out_specs=[
            pl.BlockSpec((ts, D_RNN), lambda b, t: (b * nt + t, 0)),
            pl.BlockSpec((None, 1, D_RNN), lambda b, t: (b, 0, 0)),
            pl.BlockSpec((None, 8, D_RNN), lambda b, t: (b, 0, 0)),
        ],
        out_shape=[
            jax.ShapeDtypeStruct((n_seq * seq, D_RNN), F32),
            jax.ShapeDtypeStruct((n_seq, 1, D_RNN), F32),
            jax.ShapeDtypeStruct((n_seq, 8, D_RNN), F32),
        ],
        scratch_shapes=[pltpu.VMEM((ts + 8, D_RNN), F32), pltpu.VMEM((ts, D_RNN), F32),
                        pltpu.VMEM((ts, D_RNN), F32), pltpu.VMEM((8, D_RNN), F32)],
        compiler_params=_cparams(("arbitrary", "arbitrary")),
        name="lru",
    )(u, buf0, h0, cw, cb, wa, ba, wi, bi, lam)


def _dispatch(info, counts_row):
    t_all = info.shape[0]
    eid = info[:, _R_E1:_R_E2 + 1].astype(jnp.int32)
    rank = info[:, _R_RANK1:_R_RANK2 + 1].astype(jnp.int32)
    counts = counts_row[0, _EXPERT_LANE0:_EXPERT_LANE0 + N_EXPERTS].astype(jnp.int32)
    pcounts = ((counts + MOE_TM - 1) // MOE_TM) * MOE_TM
    pends = jnp.cumsum(pcounts)
    pstarts = pends - pcounts
    dest = pstarts.at[eid].get(mode='promise_in_bounds') + rank
    n_blk = (t_all * TOP_K) // MOE_TM + N_EXPERTS
    blk_row0 = jnp.arange(n_blk, dtype=jnp.int32) * MOE_TM
    blk_e = jnp.minimum(jnp.sum((pends[None, :] <= blk_row0[:, None]).astype(jnp.int32), axis=1), N_EXPERTS - 1)
    n_valid = (pends[-1] // MOE_TM).astype(jnp.int32).reshape(1)
    tok = jnp.broadcast_to(jnp.arange(t_all, dtype=jnp.int32)[:, None], (t_all, TOP_K))
    row_tok = jnp.zeros((n_blk * MOE_TM,), jnp.int32).at[dest.reshape(-1)].set(
        tok.reshape(-1), mode='promise_in_bounds', unique_indices=True)
    return dest, blk_e.astype(jnp.int32), n_valid, row_tok


def kernel(x_prompt, x_sample, cache_k, cache_v, state_h, state_conv, page_table, p_prompt, p_sample, mix_norm, ffn_norm, ple_norm, w_qkv, q_norm, k_norm, lambda_q1, lambda_k1, lambda_q2, lambda_k2, sub_norm, w_o, w_in_lru, conv_w, conv_b, w_rg_a, b_rg_a, w_rg_i, b_rg_i, lru_lambda, w_out_lru, w_group, b_group, w_sub, b_sub, w_gate_up, w_down, w_ple_gate, w_ple_proj):
    batch, seq, _ = x_prompt.shape
    n_req, n_new, _ = x_sample.shape
    n_pages = page_table.shape[1]
    past_len = n_pages * PAGE_SIZE
    t_p = batch * seq
    t_s = n_req * n_new
    assert t_s == TM and t_p % TM == 0 and seq % TM == 0 and n_pages % PAGES_PER_STEP == 0
    n_p = t_p // TM
    n_attn = cache_k.shape[0]
    n_pool = cache_k.shape[1]

    h = jnp.concatenate([x_prompt.reshape(t_p, D_MODEL), x_sample.reshape(t_s, D_MODEL)], axis=0)

    pos = jnp.concatenate([jnp.tile(jnp.arange(seq, dtype=jnp.int32), batch),
                           jnp.tile(past_len + jnp.arange(n_new, dtype=jnp.int32), n_req)])
    half = ROPE_DIM // 2
    inv = ROPE_THETA ** (-(jnp.arange(half, dtype=F32) * 2.0 / ROPE_DIM))
    ang = pos.astype(F32)[:, None] * inv[None, :]
    cos_t = jnp.cos(ang).T
    sin_t = jnp.sin(ang).T

    ck_t = cache_k.transpose(0, 1, 3, 4, 5, 2).reshape(n_attn, n_pool, D_MODEL, PAGE_SIZE)
    cv_r = cache_v.reshape(n_attn, n_pool, PAGE_SIZE * N_HEADS, V_DIM)
    pt_flat = page_table.reshape(-1).astype(jnp.int32)
    pp = p_prompt.reshape(DEPTH, t_p, PLE_DIM)
    ps = p_sample.reshape(DEPTH, t_s, PLE_DIM)

    k_p, v_p, k_s, v_s, hl_p, ct_p, hl_s, ct_s = [], [], [], [], [], [], [], []
    y_p = y_s = None
    for i in range(DEPTH):
        j = i // 2
        g_mix = mix_norm[i].reshape(1, D_MODEL).astype(F32)
        if i % 2 == 0:
            lambda_init = 0.8 - 0.6 * math.exp(-0.3 * i)
            wq_t = w_qkv[j, :, :D_MODEL].T.astype(BF16)
            wk_t = w_qkv[j, :, D_MODEL:2 * D_MODEL].T.astype(BF16)
            wv = w_qkv[j, :, 2 * D_MODEL:].astype(BF16)
            qg_b = jnp.broadcast_to(q_norm[j].astype(F32)[:, None], (QK_DIM, TM))
            kg_b = jnp.broadcast_to(k_norm[j].astype(F32)[:, None], (QK_DIM, TM))
            qt_b, q_s, kt_p, kt_s, k_b, vv_p, vv_s, vt_b = _qkv(h, g_mix, wq_t, wk_t, wv, qg_b, kg_b, cos_t, sin_t,
                                                                n_p, batch, seq)
            lam_p = jnp.stack([lambda_q1[j], lambda_k1[j], lambda_q2[j], lambda_k2[j]]).astype(F32)
            sub_g = sub_norm[j].reshape(1, V_DIM).astype(F32)
            sub_g_b = jnp.broadcast_to(sub_norm[j].astype(F32)[:, None], (V_DIM, TQ))
            o_p = _flash(qt_b, k_b, vt_b, lam_p, sub_g_b, batch, seq, lambda_init)
            ktn = kt_s.reshape(D_MODEL, n_req, n_new).transpose(1, 0, 2)
            ktn = jnp.pad(ktn, ((0, 0), (0, 0), (0, PAGE_SIZE - n_new)))
            o_s = _decode(pt_flat, q_s, ktn, vv_s, lam_p, sub_g, ck_t, cv_r, j, n_req, n_new, n_pages,
                          lambda_init)
            h = _resid_matmul(h, o_p, o_s, w_o[j].astype(BF16), n_p)
            k_p.append(kt_p.reshape(batch, N_HEADS, 2, QK_DIM, seq).transpose(0, 4, 1, 2, 3))
            v_p.append(vv_p.reshape(batch, seq, N_HEADS, V_DIM))
            k_s.append(kt_s.T.reshape(n_req, n_new, N_HEADS, 2, QK_DIM))
            v_s.append(vv_s.reshape(n_req, n_new, N_HEADS, V_DIM))
        else:
            u = _norm_matmul(h, g_mix, w_in_lru[j].astype(BF16))
            cw = conv_w[j].astype(F32)
            cb = conv_b[j].reshape(1, D_RNN).astype(F32)
            wa = w_rg_a[j].astype(BF16)
            wi = w_rg_i[j].astype(BF16)
            ba = b_rg_a[j].reshape(1, D_RNN).astype(F32)
            bi = b_rg_i[j].reshape(1, D_RNN).astype(F32)
            lam = lru_lambda[j].reshape(1, D_RNN).astype(F32)
            zb = jnp.zeros((batch, 8, D_RNN), F32)
            zh = jnp.zeros((batch, 1, D_RNN), F32)
            yl_p, hlp, ctp = _lru(u, zb, zh, cw, cb, wa, ba, wi, bi, lam, batch, seq, LRU_TS, 0)
            buf_s = jnp.pad(state_conv[j].astype(F32), ((0, 0), (8 - (CONV_W - 1), 0), (0, 0)))
            h0_s = state_h[j].astype(F32)[:, None, :]
            yl_s, hls, cts = _lru(u, buf_s, h0_s, cw, cb, wa, ba, wi, bi, lam, n_req, n_new, n_new,
                                  t_p // n_new)
            h = _resid_matmul(h, yl_p, yl_s, w_out_lru[j].astype(BF16), n_p)
            hl_p.append(hlp[:, 0])
            ct_p.append(ctp[:, 8 - (CONV_W - 1):])
            hl_s.append(hls[:, 0])
            ct_s.append(cts[:, 8 - (CONV_W - 1):])

        w_r = jnp.concatenate([w_group[i], w_sub[i]], axis=1).astype(F32)
        w_r = jnp.pad(w_r, ((0, 0), (0, 128 - w_r.shape[1])))
        w_hi = w_r.astype(BF16)
        w_lo = (w_r - w_hi.astype(F32)).astype(BF16)
        bias = jnp.concatenate([b_group[i], b_sub[i]]).astype(F32)
        bias = jnp.pad(bias, (0, _R_LANES - bias.shape[0])).reshape(1, _R_LANES)
        hn, info, counts = _router(h, ffn_norm[i].reshape(1, D_MODEL).astype(F32), w_hi, w_lo, bias)
        dest, blk_e, n_valid, row_tok = _dispatch(info, counts)
        xs = hn.at[row_tok].get(mode='promise_in_bounds')
        ys = _moe_ffn(blk_e, n_valid, xs, w_gate_up, w_down, i)
        ma = ys.at[dest[:, 0]].get(mode='promise_in_bounds')
        mb = ys.at[dest[:, 1]].get(mode='promise_in_bounds')

        last = i == DEPTH - 1
        out = _ple(h, ma, mb, info, ple_norm[i].reshape(1, D_MODEL).astype(F32), w_ple_gate[i].astype(BF16),
                   pp, ps, w_ple_proj[i].astype(BF16), i, n_p, last)
        if last:
            y_p, y_s = out
        else:
            h = out

    return (y_p.reshape(batch, seq, D_MODEL), y_s.reshape(n_req, n_new, D_MODEL),
            jnp.stack(k_p), jnp.stack(v_p), jnp.stack(hl_p), jnp.stack(ct_p),
            jnp.stack(k_s), jnp.stack(v_s), jnp.stack(hl_s), jnp.stack(ct_s))
```

```python
import functools
import math

import jax
import jax.numpy as jnp
import numpy as np
from jax import lax
from jax.experimental import pallas as pl
from jax.experimental.pallas import tpu as pltpu

F32 = jnp.float32
BF16 = jnp.bfloat16

D_MODEL = 1024
DEPTH = 4
PAGE_SIZE = 128
N_HEADS = 8
QK_DIM = 64
V_DIM = 128
ROPE_DIM = 16
ROPE_THETA = 500000.0
D_RNN = 1280
RG_BLOCK = 128
N_RG_BLOCKS = 10
CONV_W = 4
RG_C = 8.0
N_GROUPS = 4
EXPERTS_PER_GROUP = 8
N_EXPERTS = 32
TOP_K = 2
D_EXPERT = 512
PLE_DIM = 256
EPS = 1e-6

TM = 256
TQ = 256
PAGES_PER_STEP = 8
MOE_TM = 256
LRU_TS = 256
VMEM_LIMIT = 56 * 1024 * 1024

_NT = (((1,), (1,)), ((), ()))
_STRIP = 128
_KV_UNROLL = 4
_Q_SCALE = QK_DIM ** -0.5 * math.log2(math.e)


def _cparams(sem):
    return pltpu.CompilerParams(dimension_semantics=sem, vmem_limit_bytes=VMEM_LIMIT)


def _rms(x, g):
    return x * lax.rsqrt(jnp.mean(x * x, axis=-1, keepdims=True) + EPS) * g


def _tile(i, n_p):
    return jnp.where(i == 0, n_p, i - 1)


def _ptile(i):
    return jnp.maximum(i - 1, 0)


def _qkv_kernel(h_ref, g_ref, wq_ref, wk_ref, wv_ref, qg_ref, kg_ref, cos_ref, sin_ref, *rest):
    qtb_ref, qs_ref, ktp_ref, kts_ref, kb_ref, vp_ref, vs_ref, vtb_ref, qt_s, kt_s = rest[-10:]
    i = pl.program_id(0)
    hn = _rms(h_ref[...], g_ref[...]).astype(BF16)
    cos = cos_ref[...]
    sin = sin_ref[...]

    def normrope(w_ref, gain_ref, dst, scale):
        xt = lax.dot_general(w_ref[...], hn, _NT, preferred_element_type=F32)
        gain = gain_ref[...]
        for g in range(2 * N_HEADS):
            xg = xt[g * QK_DIM:(g + 1) * QK_DIM, :]
            ss = jnp.sum(xg * xg, axis=0, keepdims=True)
            yg = xg * lax.rsqrt(ss * (1.0 / QK_DIM) + EPS) * gain
            y1 = yg[0:8]
            y2 = yg[8:16]
            dst[g * QK_DIM:g * QK_DIM + 8, :] = (y1 * cos - y2 * sin) * scale
            dst[g * QK_DIM + 8:g * QK_DIM + 16, :] = (y2 * cos + y1 * sin) * scale
            dst[g * QK_DIM + 16:(g + 1) * QK_DIM, :] = yg[16:] * scale

    normrope(wq_ref, qg_ref, qt_s, _Q_SCALE)
    normrope(wk_ref, kg_ref, kt_s, 1.0)
    v = jnp.dot(hn, wv_ref[...], preferred_element_type=F32)

    @pl.when(i == 0)
    def _():
        qs_ref[...] = qt_s[...].T
        kts_ref[...] = kt_s[...]
        vs_ref[...] = v

    @pl.when(i > 0)
    def _():
        kt = kt_s[...]
        qtb_ref[...] = qt_s[...].astype(BF16)
        ktp_ref[...] = kt
        kb_ref[...] = kt.T.astype(BF16)
        vp_ref[...] = v
        vtb_ref[...] = v.T.astype(BF16)


def _qkv(h, g, wq_t, wk_t, wv, qg_b, kg_b, cos_t, sin_t, n_p, batch, seq, layer, n_attn, prev):
    n_sb = seq // TM
    const = lambda i: (0, 0)
    tile = lambda i: (_tile(i, n_p), 0)
    ktile = lambda i: (_ptile(i) // n_sb, 0, _ptile(i) % n_sb)
    ltile = lambda i: (layer, _ptile(i) // n_sb, 0, _ptile(i) % n_sb)
    prev = () if prev is None else tuple(prev)
    n_in = 9
    return pl.pallas_call(
        _qkv_kernel,
        grid=(n_p + 1,),
        in_specs=[
            pl.BlockSpec((TM, D_MODEL), tile),
            pl.BlockSpec((1, D_MODEL), const),
            pl.BlockSpec((D_MODEL, D_MODEL), const),
            pl.BlockSpec((D_MODEL, D_MODEL), const),
            pl.BlockSpec((D_MODEL, D_MODEL), const),
            pl.BlockSpec((QK_DIM, TM), const),
            pl.BlockSpec((QK_DIM, TM), const),
            pl.BlockSpec((8, TM), lambda i: (0, _tile(i, n_p))),
            pl.BlockSpec((8, TM), lambda i: (0, _tile(i, n_p))),
        ] + [pl.BlockSpec(memory_space=pl.ANY)] * len(prev),
        out_specs=[
            pl.BlockSpec((None, D_MODEL, TM), ktile),
            pl.BlockSpec((TM, D_MODEL), const),
            pl.BlockSpec((None, None, D_MODEL, TM), ltile),
            pl.BlockSpec((D_MODEL, TM), const),
            pl.BlockSpec((TM, D_MODEL), lambda i: (_ptile(i), 0)),
            pl.BlockSpec((None, TM, D_MODEL), lambda i: (layer, _ptile(i), 0)),
            pl.BlockSpec((TM, D_MODEL), const),
            pl.BlockSpec((None, D_MODEL, TM), ktile),
        ],
        out_shape=[
            jax.ShapeDtypeStruct((batch, D_MODEL, seq), BF16),
            jax.ShapeDtypeStruct((TM, D_MODEL), F32),
            jax.ShapeDtypeStruct((n_attn, batch, D_MODEL, seq), F32),
            jax.ShapeDtypeStruct((D_MODEL, TM), F32),
            jax.ShapeDtypeStruct((n_p * TM, D_MODEL), BF16),
            jax.ShapeDtypeStruct((n_attn, n_p * TM, D_MODEL), F32),
            jax.ShapeDtypeStruct((TM, D_MODEL), F32),
            jax.ShapeDtypeStruct((batch, D_MODEL, seq), BF16),
        ],
        input_output_aliases={n_in: 2, n_in + 1: 5} if prev else {},
        scratch_shapes=[pltpu.VMEM((D_MODEL, TM), F32), pltpu.VMEM((D_MODEL, TM), F32)],
        compiler_params=_cparams(("arbitrary",)),
        name="qkv",
    )(h, g, wq_t, wk_t, wv, qg_b, kg_b, cos_t, sin_t, *prev)


def _resid_matmul_kernel(h_ref, xp_ref, xs_ref, w_ref, o_ref):
    i = pl.program_id(0)
    x = jnp.where(i == 0, xs_ref[...], xp_ref[...]).astype(BF16)
    o_ref[...] = h_ref[...] + jnp.dot(x, w_ref[...], preferred_element_type=F32)


def _resid_matmul(h, x_p, x_s, w, n_p):
    t_all = h.shape[0]
    k = w.shape[0]
    tile = lambda i: (_tile(i, n_p), 0)
    return pl.pallas_call(
        _resid_matmul_kernel,
        grid=(n_p + 1,),
        in_specs=[
            pl.BlockSpec((TM, D_MODEL), tile),
            pl.BlockSpec((TM, k), lambda i: (_ptile(i), 0)),
            pl.BlockSpec((TM, k), lambda i: (0, 0)),
            pl.BlockSpec((k, D_MODEL), lambda i: (0, 0)),
        ],
        out_specs=pl.BlockSpec((TM, D_MODEL), tile),
        out_shape=jax.ShapeDtypeStruct((t_all, D_MODEL), F32),
        compiler_params=_cparams(("arbitrary",)),
        name="resid_matmul",
    )(h, x_p, x_s, w)


def _norm_matmul_kernel(h_ref, g_ref, w_ref, o_ref):
    hn = _rms(h_ref[...], g_ref[...]).astype(BF16)
    o_ref[...] = jnp.dot(hn, w_ref[...], preferred_element_type=F32)


def _norm_matmul(h, g, w):
    t_all = h.shape[0]
    n = w.shape[1]
    return pl.pallas_call(
        _norm_matmul_kernel,
        grid=(t_all // TM,),
        in_specs=[
            pl.BlockSpec((TM, D_MODEL), lambda i: (i, 0)),
            pl.BlockSpec((1, D_MODEL), lambda i: (0, 0)),
            pl.BlockSpec((D_MODEL, n), lambda i: (0, 0)),
        ],
        out_specs=pl.BlockSpec((TM, n), lambda i: (i, 0)),
        out_shape=jax.ShapeDtypeStruct((t_all, n), F32),
        compiler_params=_cparams(("arbitrary",)),
        name="norm_matmul",
    )(h, g, w)


_R_E1, _R_E2, _R_W1, _R_W2, _R_RANK1, _R_RANK2 = range(6)
_R_LANES = 128
_EXPERT_LANE0 = N_GROUPS


def _router_kernel(h_ref, g_ref, whi_ref, wlo_ref, bias_ref, hn_ref, info_ref, cnt_ref, cnt_s):
    i = pl.program_id(0)

    @pl.when(i == 0)
    def _():
        cnt_s[...] = jnp.zeros_like(cnt_s)

    hn = _rms(h_ref[...], g_ref[...])
    hi = hn.astype(BF16)
    bits = pltpu.bitcast(hi.astype(F32), jnp.uint32)
    hn_ref[...] = (bits[:, :D_MODEL // 2] >> 16) | bits[:, D_MODEL // 2:]
    lo = (hn - hi.astype(F32)).astype(BF16)
    whi = whi_ref[...]
    lg = jnp.dot(hi, whi, preferred_element_type=F32)
    lg += jnp.dot(lo, whi, preferred_element_type=F32)
    lg += jnp.dot(hi, wlo_ref[...], preferred_element_type=F32)
    lg += bias_ref[...]

    lane = lax.broadcasted_iota(jnp.int32, lg.shape, 1).astype(F32)
    far = float(_R_LANES)
    rmax = lambda x: jnp.max(x, axis=-1, keepdims=True)
    rmin = lambda x: jnp.min(x, axis=-1, keepdims=True)
    rsum = lambda x: jnp.sum(x, axis=-1, keepdims=True)

    gl = jnp.where(lane < N_GROUPS, lg, -jnp.inf)
    gmax = rmax(gl)
    ge = jnp.exp(gl - gmax)
    gp = ge / rsum(ge)
    g_idx = rmin(jnp.where(gl == gmax, lane, far))
    g_w = rsum(jnp.where(lane == g_idx, gp, 0.0))

    lane0 = _EXPERT_LANE0 + g_idx * EXPERTS_PER_GROUP
    smask = (lane >= lane0) & (lane < lane0 + EXPERTS_PER_GROUP)
    sl = jnp.where(smask, lg, -jnp.inf)
    se = jnp.exp(sl - rmax(sl))
    sp = jnp.where(smask, se / rsum(se), -1.0)
    v1 = rmax(sp)
    i1 = rmin(jnp.where(sp == v1, lane, far))
    sp2 = jnp.where(lane == i1, -1.0, sp)
    v2 = rmax(sp2)
    i2 = rmin(jnp.where(sp2 == v2, lane, far))
    den = v1 + v2
    w1 = g_w * (v1 / den)
    w2 = g_w * (v2 / den)

    oh1 = jnp.where(lane == i1, 1.0, 0.0)
    oh2 = jnp.where(lane == i2, 1.0, 0.0)
    r = lax.broadcasted_iota(jnp.int32, (TM, TM), 0)
    c = lax.broadcasted_iota(jnp.int32, (TM, TM), 1)
    tri = jnp.where(c < r, 1.0, 0.0).astype(BF16)
    pre1 = jnp.dot(tri, oh1.astype(BF16), preferred_element_type=F32)
    pre2 = jnp.dot(tri, oh2.astype(BF16), preferred_element_type=F32)
    c1 = jnp.sum(oh1, axis=0, keepdims=True)
    c2 = jnp.sum(oh2, axis=0, keepdims=True)
    base = cnt_s[...]
    rank1 = rsum(oh1 * (pre1 + base))
    rank2 = rsum(oh2 * (pre2 + (base + c1)))
    total = base + (c1 + c2)
    cnt_s[...] = total
    cnt_ref[...] = total

    info = jnp.zeros_like(lg)
    for k, val in ((_R_E1, i1 - _EXPERT_LANE0), (_R_E2, i2 - _EXPERT_LANE0), (_R_W1, w1), (_R_W2, w2),
                   (_R_RANK1, rank1), (_R_RANK2, rank2)):
        info = jnp.where(lane == float(k), val, info)
    info_ref[...] = info


def _router(h, g, w_hi, w_lo, bias):
    t_all = h.shape[0]
    const = lambda i: (0, 0)
    return pl.pallas_call(
        _router_kernel,
        grid=(t_all // TM,),
        in_specs=[
            pl.BlockSpec((TM, D_MODEL), lambda i: (i, 0)),
            pl.BlockSpec((1, D_MODEL), const),
            pl.BlockSpec((D_MODEL, _R_LANES), const),
            pl.BlockSpec((D_MODEL, _R_LANES), const),
            pl.BlockSpec((1, _R_LANES), const),
        ],
        out_specs=[
            pl.BlockSpec((TM, D_MODEL // 2), lambda i: (i, 0)),
            pl.BlockSpec((TM, _R_LANES), lambda i: (i, 0)),
            pl.BlockSpec((1, _R_LANES), const),
        ],
        out_shape=[
            jax.ShapeDtypeStruct((t_all, D_MODEL // 2), jnp.uint32),
            jax.ShapeDtypeStruct((t_all, _R_LANES), F32),
            jax.ShapeDtypeStruct((1, _R_LANES), F32),
        ],
        scratch_shapes=[pltpu.VMEM((1, _R_LANES), F32)],
        compiler_params=_cparams(("arbitrary",)),
        name="router",
    )(h, g, w_hi, w_lo, bias)


def _moe_ffn_kernel(be_ref, nv_ref, x_ref, wgu_ref, wd_ref, o_ref, wgu_s, wd_s):
    j = pl.program_id(0)
    valid = j < nv_ref[0]
    new_expert = (j == 0) | (be_ref[j] != be_ref[jnp.maximum(j - 1, 0)])

    @pl.when(valid & new_expert)
    def _():
        wgu_s[...] = wgu_ref[...].astype(BF16)
        wd_s[...] = wd_ref[...].astype(BF16)

    @pl.when(valid)
    def _():
        words = x_ref[...]
        x = jnp.concatenate([pltpu.bitcast(words << 16, F32),
                             pltpu.bitcast(words & jnp.uint32(0xFFFF0000), F32)], axis=1).astype(BF16)
        gu = jnp.dot(x, wgu_s[...], preferred_element_type=F32)
        act = jax.nn.silu(gu[:, :D_EXPERT]) * gu[:, D_EXPERT:]
        o_ref[...] = jnp.dot(act.astype(BF16), wd_s[...], preferred_element_type=F32)

    @pl.when(jnp.logical_not(valid))
    def _():
        o_ref[...] = jnp.zeros_like(o_ref)


def _moe_ffn(blk_e, n_valid, xs, wgu, wd, layer):
    n_rows = xs.shape[0]
    n_blk = n_rows // MOE_TM
    return pl.pallas_call(
        _moe_ffn_kernel,
        grid_spec=pltpu.PrefetchScalarGridSpec(
            num_scalar_prefetch=2,
            grid=(n_blk,),
            in_specs=[
                pl.BlockSpec((MOE_TM, D_MODEL // 2), lambda j, be, nv: (j, 0)),
                pl.BlockSpec((None, None, D_MODEL, 2 * D_EXPERT), lambda j, be, nv: (layer, be[j], 0, 0)),
                pl.BlockSpec((None, None, D_EXPERT, D_MODEL), lambda j, be, nv: (layer, be[j], 0, 0)),
            ],
            out_specs=pl.BlockSpec((MOE_TM, D_MODEL), lambda j, be, nv: (j, 0)),
            scratch_shapes=[pltpu.VMEM((D_MODEL, 2 * D_EXPERT), BF16), pltpu.VMEM((D_EXPERT, D_MODEL), BF16)],
        ),
        out_shape=jax.ShapeDtypeStruct((n_rows, D_MODEL), F32),
        compiler_params=_cparams(("arbitrary",)),
        name="moe_ffn",
    )(blk_e, n_valid, xs, wgu, wd)


def _ple_kernel(h_ref, ma_ref, mb_ref, info_ref, g_ref, wg_ref, pp_ref, ps_ref, wp_ref, *o_refs, split):
    i = pl.program_id(0)
    info = info_ref[...]
    w1 = info[:, _R_W1:_R_W1 + 1]
    w2 = info[:, _R_W2:_R_W2 + 1]
    x = h_ref[...] + (w1 * ma_ref[...] + w2 * mb_ref[...])
    hn = _rms(x, g_ref[...]).astype(BF16)
    gate = jax.nn.sigmoid(jnp.dot(hn, wg_ref[...], preferred_element_type=F32))
    p = jnp.where(i == 0, ps_ref[...], pp_ref[...]).astype(BF16)
    out = x + gate * jnp.dot(p, wp_ref[...], preferred_element_type=F32)
    if split:
        yp_ref, ys_ref = o_refs

        @pl.when(i == 0)
        def _():
            ys_ref[...] = out

        @pl.when(i > 0)
        def _():
            yp_ref[...] = out
    else:
        o_refs[0][...] = out


def _ple(h, ma, mb, info, g, wg, p_p, p_s, wp, layer, n_p, split):
    t_all = h.shape[0]
    tile = lambda i: (_tile(i, n_p), 0)
    const = lambda i: (0, 0)
    if split:
        out_specs = [pl.BlockSpec((TM, D_MODEL), lambda i: (_ptile(i), 0)),
                     pl.BlockSpec((TM, D_MODEL), const)]
        out_shape = [jax.ShapeDtypeStruct((n_p * TM, D_MODEL), F32),
                     jax.ShapeDtypeStruct((TM, D_MODEL), F32)]
    else:
        out_specs = pl.BlockSpec((TM, D_MODEL), tile)
        out_shape = jax.ShapeDtypeStruct((t_all, D_MODEL), F32)
    return pl.pallas_call(
        functools.partial(_ple_kernel, split=split),
        grid=(n_p + 1,),
        in_specs=[
            pl.BlockSpec((TM, D_MODEL), tile),
            pl.BlockSpec((TM, D_MODEL), tile),
            pl.BlockSpec((TM, D_MODEL), tile),
            pl.BlockSpec((TM, _R_LANES), tile),
            pl.BlockSpec((1, D_MODEL), const),
            pl.BlockSpec((D_MODEL, D_MODEL), const),
            pl.BlockSpec((None, TM, PLE_DIM), lambda i: (layer, _ptile(i), 0)),
            pl.BlockSpec((None, TM, PLE_DIM), lambda i: (layer, 0, 0)),
            pl.BlockSpec((PLE_DIM, D_MODEL), const),
        ],
        out_specs=out_specs,
        out_shape=out_shape,
        compiler_params=_cparams(("arbitrary",)),
        name="ple",
    )(h, ma, mb, info, g, wg, p_p, p_s, wp)


def _lambda(lp_ref, lambda_init):
    lp = lp_ref[...]
    s1 = jnp.sum(lp[0:1] * lp[1:2], axis=-1, keepdims=True)
    s2 = jnp.sum(lp[2:3] * lp[3:4], axis=-1, keepdims=True)
    return jnp.exp(s1) - jnp.exp(s2) + lambda_init


def _head_out(o0, o1, lam, sg, lambda_init):
    o = o0 - lam * o1
    return _rms(o, sg) * (1.0 - lambda_init)


def _flash_kernel(qt_ref, k_ref, vt_ref, lp_ref, sg_ref, o_ref, acc_s, *, lambda_init):
    qi = pl.program_id(2)
    qt = qt_ref[...]
    zero = jnp.zeros((QK_DIM, TQ), qt.dtype)
    q2t = jnp.concatenate([jnp.concatenate([qt[:QK_DIM], zero], axis=0),
                           jnp.concatenate([zero, qt[QK_DIM:]], axis=0)], axis=1)
    acc_s[...] = jnp.zeros_like(acc_s)
    n_strip = 2 * TQ // _STRIP

    def attend(j, carry, masked):
        c0 = pl.multiple_of(j * TQ, TQ)
        kj = k_ref[pl.ds(c0, TQ), :]
        vtj = vt_ref[:, pl.ds(c0, TQ)]
        out = []
        for c in range(n_strip):
            cs = slice(c * _STRIP, (c + 1) * _STRIP)
            m_prev, l_prev = carry[c]
            st = jnp.dot(kj, q2t[:, cs], preferred_element_type=F32)
            if masked:
                key = lax.broadcasted_iota(jnp.int32, st.shape, 0)
                tok = lax.broadcasted_iota(jnp.int32, st.shape, 1) + (c * _STRIP) % TQ
                st = jnp.where(key <= tok, st, -jnp.inf)
            m_new = jnp.maximum(m_prev, jnp.max(st, axis=0, keepdims=True))
            alpha = jnp.exp2(m_prev - m_new)
            p = jnp.exp2(st - m_new)
            l_new = alpha * l_prev + jnp.sum(p, axis=0, keepdims=True)
            pv = jnp.dot(vtj, p.astype(BF16), preferred_element_type=F32)
            acc_s[:, cs] = alpha * acc_s[:, cs] + pv
            out.append((m_new, l_new))
        return tuple(out)

    def group(jj, carry):
        for u in range(_KV_UNROLL):
            carry = attend(_KV_UNROLL * jj + u, carry, False)
        return carry

    init = tuple((jnp.full((1, _STRIP), -jnp.inf, F32), jnp.zeros((1, _STRIP), F32)) for _ in range(n_strip))
    n_grp = qi // _KV_UNROLL
    carry = lax.fori_loop(0, n_grp, group, init)
    carry = lax.fori_loop(n_grp * _KV_UNROLL, qi, lambda j, c: attend(j, c, False), carry)
    carry = attend(qi, carry, True)

    on = acc_s[...] * (1.0 / jnp.concatenate([l for _, l in carry], axis=1))
    lam = _lambda(lp_ref, lambda_init)
    ot = on[:, :TQ] - lam * on[:, TQ:]
    y = ot * lax.rsqrt(jnp.mean(ot * ot, axis=0, keepdims=True) + EPS) * sg_ref[...]
    o_ref[...] = (y * (1.0 - lambda_init)).T


def _flash(qt_b, k_b, vt_b, lam_p, sub_g_b, batch, seq, lambda_init):
    n_q = seq // TQ
    return pl.pallas_call(
        functools.partial(_flash_kernel, lambda_init=lambda_init),
        grid=(batch, N_HEADS, n_q),
        in_specs=[
            pl.BlockSpec((None, V_DIM, TQ), lambda b, h, i: (b, h, i)),
            pl.BlockSpec((seq, V_DIM), lambda b, h, i: (b, h)),
            pl.BlockSpec((None, V_DIM, seq), lambda b, h, i: (b, h, 0)),
            pl.BlockSpec((4, QK_DIM), lambda b, h, i: (0, 0)),
            pl.BlockSpec((V_DIM, TQ), lambda b, h, i: (0, 0)),
        ],
        out_specs=pl.BlockSpec((TQ, V_DIM), lambda b, h, i: (b * n_q + i, h)),
        out_shape=jax.ShapeDtypeStruct((batch * seq, D_MODEL), F32),
        scratch_shapes=[pltpu.VMEM((V_DIM, 2 * TQ), F32)],
        compiler_params=_cparams(("arbitrary", "arbitrary", "arbitrary")),
        name="flash",
    )(qt_b, k_b, vt_b, lam_p, sub_g_b)


def _decode_kernel(pt_ref, q_ref, ktn_ref, vn_ref, lp_ref, sg_ref, *rest, lambda_init, n_new):
    del pt_ref
    npg = PAGES_PER_STEP
    k_refs = rest[:npg]
    v_refs = rest[npg:2 * npg]
    o_ref = rest[2 * npg]
    q2_s, m_s, l_s, acc_s = rest[2 * npg + 1:]
    p = pl.program_id(1)

    @pl.when(p == 0)
    def _():
        qq = jnp.concatenate([q_ref[...]] * (2 * N_HEADS), axis=0)
        row = lax.broadcasted_iota(jnp.int32, qq.shape, 0)
        lane = lax.broadcasted_iota(jnp.int32, qq.shape, 1)
        q2_s[...] = jnp.where(lane // QK_DIM == row // n_new, qq, 0.0).astype(BF16)
        m_s[...] = jnp.full_like(m_s, -jnp.inf)
        l_s[...] = jnp.zeros_like(l_s)
        acc_s[...] = jnp.zeros_like(acc_s)

    def update(s, v):
        m_prev = m_s[...]
        m_new = jnp.maximum(m_prev, jnp.max(s, axis=-1, keepdims=True))
        alpha = jnp.exp2(m_prev - m_new)
        pr = jnp.exp2(s - m_new)
        l_s[...] = alpha * l_s[...] + jnp.sum(pr, axis=-1, keepdims=True)
        acc_s[...] = alpha * acc_s[...] + jnp.dot(pr.astype(BF16), v, preferred_element_type=F32)
        m_s[...] = m_new

    q2 = q2_s[...]
    s = jnp.concatenate(
        [jnp.dot(q2, k_refs[i][...].astype(BF16), preferred_element_type=F32) for i in range(npg)], axis=1)
    v = jnp.concatenate(
        [jnp.concatenate([v_refs[i][pl.ds(h, PAGE_SIZE, stride=N_HEADS), :] for h in range(N_HEADS)], axis=1)
         for i in range(npg)], axis=0).astype(BF16)
    update(s, v)

    @pl.when(p == pl.num_programs(1) - 1)
    def _():
        sn = jnp.dot(q2, ktn_ref[...].astype(BF16), preferred_element_type=F32)
        row = lax.broadcasted_iota(jnp.int32, sn.shape, 0)
        col = lax.broadcasted_iota(jnp.int32, sn.shape, 1)
        sn = jnp.where(col <= row % n_new, sn, -jnp.inf)
        vn = jnp.concatenate([vn_ref[...], jnp.zeros((PAGE_SIZE - n_new, D_MODEL), F32)], axis=0).astype(BF16)
        update(sn, vn)
        lam = _lambda(lp_ref, lambda_init)
        on = acc_s[...] / l_s[...]
        for h in range(N_HEADS):
            blk = on[h * 2 * n_new:(h + 1) * 2 * n_new, h * V_DIM:(h + 1) * V_DIM]
            o_ref[:, h * V_DIM:(h + 1) * V_DIM] = _head_out(blk[:n_new], blk[n_new:], lam, sg_ref[...],
                                                             lambda_init)


def _decode(pt_flat, q, ktn, v_s, lam_p, sub_g, ck_t, cv_r, layer, n_req, n_new, n_pages, lambda_init):
    npg = PAGES_PER_STEP
    n_steps = n_pages // npg
    n_cols = 2 * N_HEADS * n_new

    def page_spec(i):
        return pl.BlockSpec((None, None, D_MODEL, PAGE_SIZE),
                            lambda b, p, pt: (layer, pt[b * n_pages + p * npg + i], 0, 0))

    return pl.pallas_call(
        functools.partial(_decode_kernel, lambda_init=lambda_init, n_new=n_new),
        grid_spec=pltpu.PrefetchScalarGridSpec(
            num_scalar_prefetch=1,
            grid=(n_req, n_steps),
            in_specs=[
                pl.BlockSpec((n_new, D_MODEL), lambda b, p, pt: (b, 0)),
                pl.BlockSpec((None, D_MODEL, PAGE_SIZE), lambda b, p, pt: (b, 0, 0)),
                pl.BlockSpec((n_new, D_MODEL), lambda b, p, pt: (b, 0)),
                pl.BlockSpec((4, QK_DIM), lambda b, p, pt: (0, 0)),
                pl.BlockSpec((1, V_DIM), lambda b, p, pt: (0, 0)),
            ] + [page_spec(i) for i in range(npg)] + [page_spec(i) for i in range(npg)],
            out_specs=pl.BlockSpec((n_new, D_MODEL), lambda b, p, pt: (b, 0)),
            scratch_shapes=[pltpu.VMEM((n_cols, D_MODEL), BF16), pltpu.VMEM((n_cols, 1), F32),
                            pltpu.VMEM((n_cols, 1), F32), pltpu.VMEM((n_cols, D_MODEL), F32)],
        ),
        out_shape=jax.ShapeDtypeStruct((n_req * n_new, D_MODEL), F32),
        compiler_params=_cparams(("arbitrary", "arbitrary")),
        name="decode_attn",
    )(pt_flat, q, ktn, v_s, lam_p, sub_g, *([ck_t] * npg), *([cv_r] * npg))


def _lru_kernel(u_ref, buf_ref, h0_ref, cw_ref, cb_ref, wa_ref, ba_ref, wi_ref, bi_ref, lam_ref,
                y_ref, hl_ref, ct_ref, xbuf, a_s, b_s, hc, *, ts):
    t = pl.program_id(1)

    @pl.when(t == 0)
    def _():
        xbuf[0:8, :] = buf_ref[...]
        hc[...] = jnp.broadcast_to(h0_ref[...], (8, D_RNN))

    xr = u_ref[:, D_RNN:]
    xbuf[8:8 + ts, :] = xr
    cw = cw_ref[...]
    xc = cb_ref[...] + (cw[0:1] * xbuf[5:5 + ts, :] + cw[1:2] * xbuf[6:6 + ts, :]
                        + cw[2:3] * xbuf[7:7 + ts, :] + cw[3:4] * xr)
    z = -lam_ref[...]
    sp = jnp.maximum(z, 0.0) + jnp.log1p(jnp.exp(-jnp.abs(z)))
    row = lax.broadcasted_iota(jnp.int32, (ts, RG_BLOCK), 0) % 8
    for n in range(N_RG_BLOCKS):
        sl = slice(n * RG_BLOCK, (n + 1) * RG_BLOCK)
        xcn = xc[:, sl]
        xb = xcn.astype(BF16)
        r = jax.nn.sigmoid(jnp.dot(xb, wa_ref[n], preferred_element_type=F32) + ba_ref[:, sl])
        ig = jax.nn.sigmoid(jnp.dot(xb, wi_ref[n], preferred_element_type=F32) + bi_ref[:, sl])
        log_a = -RG_C * r * sp[:, sl]
        a = jnp.exp(log_a)
        mult = jnp.sqrt(-jnp.tanh(log_a) * (a * a + 1.0))
        b = mult * (ig * xcn)
        for s in (1, 2, 4):
            a_sh = pltpu.roll(a, s, 0)
            b_sh = pltpu.roll(b, s, 0)
            keep = row >= s
            b = jnp.where(keep, a * b_sh + b, b)
            a = jnp.where(keep, a * a_sh, a)
        a_s[:, sl] = a
        b_s[:, sl] = b

    def step(i, h):
        r0 = pl.multiple_of(i * 8, 8)
        ht = a_s[pl.ds(r0, 8), :] * h + b_s[pl.ds(r0, 8), :]
        b_s[pl.ds(r0, 8), :] = ht
        return jnp.broadcast_to(ht[7:8, :], (8, D_RNN))

    h = lax.fori_loop(0, ts // 8, step, hc[...])
    hc[...] = h
    xbuf[0:8, :] = xbuf[ts:ts + 8, :]
    y_ref[...] = b_s[...] * jax.nn.gelu(u_ref[:, :D_RNN])

    @pl.when(t == pl.num_programs(1) - 1)
    def _():
        hl_ref[...] = h[0:1]
        ct_ref[...] = xbuf[0:8, :]


def _lru(u, buf0, h0, cw, cb, wa, ba, wi, bi, lam, n_seq, seq, ts, row_blk0):
    nt = seq // ts
    c2 = lambda b, t: (0, 0)
    c3 = lambda b, t: (0, 0, 0)
    return pl.pallas_call(
        functools.partial(_lru_kernel, ts=ts),
        grid=(n_seq, nt),
        in_specs=[
            pl.BlockSpec((ts, 2 * D_RNN), lambda b, t: (row_blk0 + b * nt + t, 0)),
            pl.BlockSpec((None, 8, D_RNN), lambda b, t: (b, 0, 0)),
            pl.BlockSpec((None, 1, D_RNN), lambda b, t: (b, 0, 0)),
            pl.BlockSpec((CONV_W, D_RNN), c2),
            pl.BlockSpec((1, D_RNN), c2),
            pl.BlockSpec((N_RG_BLOCKS, RG_BLOCK, RG_BLOCK), c3),
            pl.BlockSpec((1, D_RNN), c2),
            pl.BlockSpec((N_RG_BLOCKS, RG_BLOCK, RG_BLOCK), c3),
            pl.BlockSpec((1, D_RNN), c2),
            pl.BlockSpec((1, D_RNN), c2),
        ],
        out_specs=[
            pl.BlockSpec((ts, D_RNN), lambda b, t: (b * nt + t, 0)),
            pl.BlockSpec((None, 1, D_RNN), lambda b, t: (b, 0, 0)),
            pl.BlockSpec((None, 8, D_RNN), lambda b, t: (b, 0, 0)),
        ],
        out_shape=[
            jax.ShapeDtypeStruct((n_seq * seq, D_RNN), F32),
            jax.ShapeDtypeStruct((n_seq, 1, D_RNN), F32),
            jax.ShapeDtypeStruct((n_seq, 8, D_RNN), F32),
        ],
        scratch_shapes=[pltpu.VMEM((ts + 8, D_RNN), F32), pltpu.VMEM((ts, D_RNN), F32),
                        pltpu.VMEM((ts, D_RNN), F32), pltpu.VMEM((8, D_RNN), F32)],
        compiler_params=_cparams(("arbitrary", "arbitrary")),
        name="lru",
    )(u, buf0, h0, cw, cb, wa, ba, wi, bi, lam)


def _dispatch(info, counts_row):
    t_all = info.shape[0]
    eid = info[:, _R_E1:_R_E2 + 1].astype(jnp.int32)
    rank = info[:, _R_RANK1:_R_RANK2 + 1].astype(jnp.int32)
    counts = counts_row[0, _EXPERT_LANE0:_EXPERT_LANE0 + N_EXPERTS].astype(jnp.int32)
    pcounts = ((counts + MOE_TM - 1) // MOE_TM) * MOE_TM
    pends = jnp.cumsum(pcounts)
    pstarts = pends - pcounts
    dest = pstarts.at[eid].get(mode='promise_in_bounds') + rank
    n_blk = (t_all * TOP_K) // MOE_TM + N_EXPERTS
    blk_row0 = jnp.arange(n_blk, dtype=jnp.int32) * MOE_TM
    blk_e = jnp.minimum(jnp.sum((pends[None, :] <= blk_row0[:, None]).astype(jnp.int32), axis=1), N_EXPERTS - 1)
    n_valid = (pends[-1] // MOE_TM).astype(jnp.int32).reshape(1)
    tok = jnp.broadcast_to(jnp.arange(t_all, dtype=jnp.int32)[:, None], (t_all, TOP_K))
    row_tok = (jnp.arange(n_blk * MOE_TM, dtype=jnp.int32) % t_all).at[dest.reshape(-1)].set(
        tok.reshape(-1), mode='promise_in_bounds', unique_indices=True)
    return dest, blk_e.astype(jnp.int32), n_valid, row_tok


def kernel(x_prompt, x_sample, cache_k, cache_v, state_h, state_conv, page_table, p_prompt, p_sample, mix_norm, ffn_norm, ple_norm, w_qkv, q_norm, k_norm, lambda_q1, lambda_k1, lambda_q2, lambda_k2, sub_norm, w_o, w_in_lru, conv_w, conv_b, w_rg_a, b_rg_a, w_rg_i, b_rg_i, lru_lambda, w_out_lru, w_group, b_group, w_sub, b_sub, w_gate_up, w_down, w_ple_gate, w_ple_proj):
    batch, seq, _ = x_prompt.shape
    n_req, n_new, _ = x_sample.shape
    n_pages = page_table.shape[1]
    past_len = n_pages * PAGE_SIZE
    t_p = batch * seq
    t_s = n_req * n_new
    assert t_s == TM and t_p % TM == 0 and seq % TM == 0 and n_pages % PAGES_PER_STEP == 0
    n_p = t_p // TM
    n_attn = cache_k.shape[0]
    n_pool = cache_k.shape[1]

    h = jnp.concatenate([x_prompt.reshape(t_p, D_MODEL), x_sample.reshape(t_s, D_MODEL)], axis=0)

    pos = jnp.concatenate([jnp.tile(jnp.arange(seq, dtype=jnp.int32), batch),
                           jnp.tile(past_len + jnp.arange(n_new, dtype=jnp.int32), n_req)])
    half = ROPE_DIM // 2
    inv = ROPE_THETA ** (-(jnp.arange(half, dtype=F32) * 2.0 / ROPE_DIM))
    ang = pos.astype(F32)[:, None] * inv[None, :]
    cos_t = jnp.cos(ang).T
    sin_t = jnp.sin(ang).T

    ck_t = cache_k.transpose(0, 1, 3, 4, 5, 2).reshape(n_attn, n_pool, D_MODEL, PAGE_SIZE)
    cv_r = cache_v.reshape(n_attn, n_pool, PAGE_SIZE * N_HEADS, V_DIM)
    pt_flat = page_table.reshape(-1).astype(jnp.int32)
    pp = p_prompt.reshape(DEPTH, t_p, PLE_DIM)
    ps = p_sample.reshape(DEPTH, t_s, PLE_DIM)

    k_s, v_s, hl_p, ct_p, hl_s, ct_s = [], [], [], [], [], []
    y_p = y_s = kv_prev = None
    for i in range(DEPTH):
        j = i // 2
        g_mix = mix_norm[i].reshape(1, D_MODEL).astype(F32)
        if i % 2 == 0:
            lambda_init = 0.8 - 0.6 * math.exp(-0.3 * i)
            wq_t = w_qkv[j, :, :D_MODEL].T.astype(BF16)
            wk_t = w_qkv[j, :, D_MODEL:2 * D_MODEL].T.astype(BF16)
            wv = w_qkv[j, :, 2 * D_MODEL:].astype(BF16)
            qg_b = jnp.broadcast_to(q_norm[j].astype(F32)[:, None], (QK_DIM, TM))
            kg_b = jnp.broadcast_to(k_norm[j].astype(F32)[:, None], (QK_DIM, TM))
            qt_b, q_s, kt_p, kt_s, k_b, vv_p, vv_s, vt_b = _qkv(h, g_mix, wq_t, wk_t, wv, qg_b, kg_b, cos_t, sin_t,
                                                                n_p, batch, seq, j, n_attn, kv_prev)
            kv_prev = (kt_p, vv_p)
            lam_p = jnp.stack([lambda_q1[j], lambda_k1[j], lambda_q2[j], lambda_k2[j]]).astype(F32)
            sub_g = sub_norm[j].reshape(1, V_DIM).astype(F32)
            sub_g_b = jnp.broadcast_to(sub_norm[j].astype(F32)[:, None], (V_DIM, TQ))
            o_p = _flash(qt_b, k_b, vt_b, lam_p, sub_g_b, batch, seq, lambda_init)
            ktn = kt_s.reshape(D_MODEL, n_req, n_new).transpose(1, 0, 2)
            ktn = jnp.pad(ktn, ((0, 0), (0, 0), (0, PAGE_SIZE - n_new)))
            o_s = _decode(pt_flat, q_s, ktn, vv_s, lam_p, sub_g, ck_t, cv_r, j, n_req, n_new, n_pages,
                          lambda_init)
            h = _resid_matmul(h, o_p, o_s, w_o[j].astype(BF16), n_p)
            k_s.append(kt_s.T.reshape(n_req, n_new, N_HEADS, 2, QK_DIM))
            v_s.append(vv_s.reshape(n_req, n_new, N_HEADS, V_DIM))
        else:
            u = _norm_matmul(h, g_mix, w_in_lru[j].astype(BF16))
            cw = conv_w[j].astype(F32)
            cb = conv_b[j].reshape(1, D_RNN).astype(F32)
            wa = w_rg_a[j].astype(BF16)
            wi = w_rg_i[j].astype(BF16)
            ba = b_rg_a[j].reshape(1, D_RNN).astype(F32)
            bi = b_rg_i[j].reshape(1, D_RNN).astype(F32)
            lam = lru_lambda[j].reshape(1, D_RNN).astype(F32)
            zb = jnp.zeros((batch, 8, D_RNN), F32)
            zh = jnp.zeros((batch, 1, D_RNN), F32)
            yl_p, hlp, ctp = _lru(u, zb, zh, cw, cb, wa, ba, wi, bi, lam, batch, seq, LRU_TS, 0)
            buf_s = jnp.pad(state_conv[j].astype(F32), ((0, 0), (8 - (CONV_W - 1), 0), (0, 0)))
            h0_s = state_h[j].astype(F32)[:, None, :]
            yl_s, hls, cts = _lru(u, buf_s, h0_s, cw, cb, wa, ba, wi, bi, lam, n_req, n_new, n_new,
                                  t_p // n_new)
            h = _resid_matmul(h, yl_p, yl_s, w_out_lru[j].astype(BF16), n_p)
            hl_p.append(hlp[:, 0])
            ct_p.append(ctp[:, 8 - (CONV_W - 1):])
            hl_s.append(hls[:, 0])
            ct_s.append(cts[:, 8 - (CONV_W - 1):])

        w_r = jnp.concatenate([w_group[i], w_sub[i]], axis=1).astype(F32)
        w_r = jnp.pad(w_r, ((0, 0), (0, 128 - w_r.shape[1])))
        w_hi = w_r.astype(BF16)
        w_lo = (w_r - w_hi.astype(F32)).astype(BF16)
        bias = jnp.concatenate([b_group[i], b_sub[i]]).astype(F32)
        bias = jnp.pad(bias, (0, _R_LANES - bias.shape[0])).reshape(1, _R_LANES)
        hn, info, counts = _router(h, ffn_norm[i].reshape(1, D_MODEL).astype(F32), w_hi, w_lo, bias)
        dest, blk_e, n_valid, row_tok = _dispatch(info, counts)
        xs = hn.at[row_tok].get(mode='promise_in_bounds')
        ys = _moe_ffn(blk_e, n_valid, xs, w_gate_up, w_down, i)
        ma = ys.at[dest[:, 0]].get(mode='promise_in_bounds')
        mb = ys.at[dest[:, 1]].get(mode='promise_in_bounds')

        last = i == DEPTH - 1
        out = _ple(h, ma, mb, info, ple_norm[i].reshape(1, D_MODEL).astype(F32), w_ple_gate[i].astype(BF16),
                   pp, ps, w_ple_proj[i].astype(BF16), i, n_p, last)
        if last:
            y_p, y_s = out
        else:
            h = out

    kt_all, v_all = kv_prev
    k_prompt = kt_all.reshape(n_attn, batch, N_HEADS, 2, QK_DIM, seq).transpose(0, 1, 5, 2, 3, 4)
    v_prompt = v_all.reshape(n_attn, batch, seq, N_HEADS, V_DIM)
    return (y_p.reshape(batch, seq, D_MODEL), y_s.reshape(n_req, n_new, D_MODEL),
            k_prompt, v_prompt, jnp.stack(hl_p), jnp.stack(ct_p),
            jnp.stack(k_s), jnp.stack(v_s), jnp.stack(hl_s), jnp.stack(ct_s))
```

```python
import functools
import math

import jax
import jax.numpy as jnp
import numpy as np
from jax import lax
from jax.experimental import pallas as pl
from jax.experimental.pallas import tpu as pltpu

F32 = jnp.float32
BF16 = jnp.bfloat16

D_MODEL = 1024
DEPTH = 4
PAGE_SIZE = 128
N_HEADS = 8
QK_DIM = 64
V_DIM = 128
ROPE_DIM = 16
ROPE_THETA = 500000.0
D_RNN = 1280
RG_BLOCK = 128
N_RG_BLOCKS = 10
CONV_W = 4
RG_C = 8.0
N_GROUPS = 4
EXPERTS_PER_GROUP = 8
N_EXPERTS = 32
TOP_K = 2
D_EXPERT = 512
PLE_DIM = 256
EPS = 1e-6

TM = 256
TM_WIDE = 640
TQ = 512
TK = 256
PAGES_PER_STEP = 8
MOE_TM = 512
LRU_TS = 256
VMEM_LIMIT = 56 * 1024 * 1024

_NT = (((1,), (1,)), ((), ()))
_STRIP = 128
_Q_SCALE = QK_DIM ** -0.5 * math.log2(math.e)


def _cparams(sem):
    return pltpu.CompilerParams(dimension_semantics=sem, vmem_limit_bytes=VMEM_LIMIT)


def _rms(x, g):
    return x * lax.rsqrt(jnp.mean(x * x, axis=-1, keepdims=True) + EPS) * g


def _tile(i, n_p):
    return jnp.where(i == 0, n_p, i - 1)


def _ptile(i):
    return jnp.maximum(i - 1, 0)


def _qkv_kernel(h_ref, g_ref, wq_ref, wk_ref, wv_ref, qg_ref, kg_ref, cos_ref, sin_ref, *rest):
    qtb_ref, qs_ref, ktp_ref, kts_ref, kb_ref, vp_ref, vs_ref, vtb_ref, qt_s, kt_s = rest[-10:]
    i = pl.program_id(0)
    hn = _rms(h_ref[...], g_ref[...]).astype(BF16)
    cos = cos_ref[...]
    sin = sin_ref[...]

    def normrope(w_ref, gain_ref, dst, scale):
        xt = lax.dot_general(w_ref[...], hn, _NT, preferred_element_type=F32)
        gain = gain_ref[...]
        for g in range(2 * N_HEADS):
            xg = xt[g * QK_DIM:(g + 1) * QK_DIM, :]
            ss = jnp.sum(xg * xg, axis=0, keepdims=True)
            yg = xg * lax.rsqrt(ss * (1.0 / QK_DIM) + EPS) * gain
            y1 = yg[0:8]
            y2 = yg[8:16]
            dst[g * QK_DIM:g * QK_DIM + 8, :] = (y1 * cos - y2 * sin) * scale
            dst[g * QK_DIM + 8:g * QK_DIM + 16, :] = (y2 * cos + y1 * sin) * scale
            dst[g * QK_DIM + 16:(g + 1) * QK_DIM, :] = yg[16:] * scale

    normrope(wq_ref, qg_ref, qt_s, _Q_SCALE)
    normrope(wk_ref, kg_ref, kt_s, 1.0)
    v = jnp.dot(hn, wv_ref[...], preferred_element_type=F32)

    @pl.when(i == 0)
    def _():
        qs_ref[...] = qt_s[...].T
        kts_ref[...] = kt_s[...]
        vs_ref[...] = v

    @pl.when(i > 0)
    def _():
        kt = kt_s[...]
        qtb_ref[...] = qt_s[...].astype(BF16)
        ktp_ref[...] = kt
        kb_ref[...] = kt.T.astype(BF16)
        vp_ref[...] = v
        vtb_ref[...] = v.T.astype(BF16)


def _qkv(h, g, wq_t, wk_t, wv, qg_b, kg_b, cos_t, sin_t, n_p, batch, seq, layer, n_attn, prev):
    n_sb = seq // TM
    const = lambda i: (0, 0)
    tile = lambda i: (_tile(i, n_p), 0)
    ktile = lambda i: (_ptile(i) // n_sb, 0, _ptile(i) % n_sb)
    ltile = lambda i: (layer, _ptile(i) // n_sb, 0, _ptile(i) % n_sb)
    prev = () if prev is None else tuple(prev)
    n_in = 9
    return pl.pallas_call(
        _qkv_kernel,
        grid=(n_p + 1,),
        in_specs=[
            pl.BlockSpec((TM, D_MODEL), tile),
            pl.BlockSpec((1, D_MODEL), const),
            pl.BlockSpec((D_MODEL, D_MODEL), const),
            pl.BlockSpec((D_MODEL, D_MODEL), const),
            pl.BlockSpec((D_MODEL, D_MODEL), const),
            pl.BlockSpec((QK_DIM, TM), const),
            pl.BlockSpec((QK_DIM, TM), const),
            pl.BlockSpec((8, TM), lambda i: (0, _tile(i, n_p))),
            pl.BlockSpec((8, TM), lambda i: (0, _tile(i, n_p))),
        ] + [pl.BlockSpec(memory_space=pl.ANY)] * len(prev),
        out_specs=[
            pl.BlockSpec((None, D_MODEL, TM), ktile),
            pl.BlockSpec((TM, D_MODEL), const),
            pl.BlockSpec((None, None, D_MODEL, TM), ltile),
            pl.BlockSpec((D_MODEL, TM), const),
            pl.BlockSpec((TM, D_MODEL), lambda i: (_ptile(i), 0)),
            pl.BlockSpec((None, TM, D_MODEL), lambda i: (layer, _ptile(i), 0)),
            pl.BlockSpec((TM, D_MODEL), const),
            pl.BlockSpec((None, D_MODEL, TM), ktile),
        ],
        out_shape=[
            jax.ShapeDtypeStruct((batch, D_MODEL, seq), BF16),
            jax.ShapeDtypeStruct((TM, D_MODEL), F32),
            jax.ShapeDtypeStruct((n_attn, batch, D_MODEL, seq), F32),
            jax.ShapeDtypeStruct((D_MODEL, TM), F32),
            jax.ShapeDtypeStruct((n_p * TM, D_MODEL), BF16),
            jax.ShapeDtypeStruct((n_attn, n_p * TM, D_MODEL), F32),
            jax.ShapeDtypeStruct((TM, D_MODEL), F32),
            jax.ShapeDtypeStruct((batch, D_MODEL, seq), BF16),
        ],
        input_output_aliases={n_in: 2, n_in + 1: 5} if prev else {},
        scratch_shapes=[pltpu.VMEM((D_MODEL, TM), F32), pltpu.VMEM((D_MODEL, TM), F32)],
        compiler_params=_cparams(("arbitrary",)),
        name="qkv",
    )(h, g, wq_t, wk_t, wv, qg_b, kg_b, cos_t, sin_t, *prev)


def _resid_matmul_kernel(h_ref, xp_ref, xs_ref, w_ref, o_ref):
    i = pl.program_id(0)
    x = jnp.where(i == 0, xs_ref[...], xp_ref[...]).astype(BF16)
    o_ref[...] = h_ref[...] + jnp.dot(x, w_ref[...], preferred_element_type=F32)


def _resid_matmul(h, x_p, x_s, w, n_p):
    t_all = h.shape[0]
    k = w.shape[0]
    tile = lambda i: (_tile(i, n_p), 0)
    return pl.pallas_call(
        _resid_matmul_kernel,
        grid=(n_p + 1,),
        in_specs=[
            pl.BlockSpec((TM, D_MODEL), tile),
            pl.BlockSpec((TM, k), lambda i: (_ptile(i), 0)),
            pl.BlockSpec((TM, k), lambda i: (0, 0)),
            pl.BlockSpec((k, D_MODEL), lambda i: (0, 0)),
        ],
        out_specs=pl.BlockSpec((TM, D_MODEL), tile),
        out_shape=jax.ShapeDtypeStruct((t_all, D_MODEL), F32),
        compiler_params=_cparams(("arbitrary",)),
        name="resid_matmul",
    )(h, x_p, x_s, w)


def _norm_matmul_kernel(h_ref, g_ref, w_ref, o_ref):
    hn = _rms(h_ref[...], g_ref[...]).astype(BF16)
    o_ref[...] = jnp.dot(hn, w_ref[...], preferred_element_type=F32)


def _norm_matmul(h, g, w):
    t_all = h.shape[0]
    n = w.shape[1]
    return pl.pallas_call(
        _norm_matmul_kernel,
        grid=(t_all // TM_WIDE,),
        in_specs=[
            pl.BlockSpec((TM_WIDE, D_MODEL), lambda i: (i, 0)),
            pl.BlockSpec((1, D_MODEL), lambda i: (0, 0)),
            pl.BlockSpec((D_MODEL, n), lambda i: (0, 0)),
        ],
        out_specs=pl.BlockSpec((TM_WIDE, n), lambda i: (i, 0)),
        out_shape=jax.ShapeDtypeStruct((t_all, n), F32),
        compiler_params=_cparams(("arbitrary",)),
        name="norm_matmul",
    )(h, g, w)


_R_E1, _R_E2, _R_W1, _R_W2, _R_RANK1, _R_RANK2 = range(6)
_R_LANES = 128
_EXPERT_LANE0 = N_GROUPS


def _router_kernel(h_ref, g_ref, whi_ref, wlo_ref, bias_ref, hn_ref, info_ref, cnt_ref, cnt_s):
    i = pl.program_id(0)

    @pl.when(i == 0)
    def _():
        cnt_s[...] = jnp.zeros_like(cnt_s)

    hn = _rms(h_ref[...], g_ref[...])
    hi = hn.astype(BF16)
    bits = pltpu.bitcast(hi.astype(F32), jnp.uint32)
    hn_ref[...] = (bits[:, :D_MODEL // 2] >> 16) | bits[:, D_MODEL // 2:]
    lo = (hn - hi.astype(F32)).astype(BF16)
    whi = whi_ref[...]
    lg = jnp.dot(hi, whi, preferred_element_type=F32)
    lg += jnp.dot(lo, whi, preferred_element_type=F32)
    lg += jnp.dot(hi, wlo_ref[...], preferred_element_type=F32)
    lg += bias_ref[...]

    lane = lax.broadcasted_iota(jnp.int32, lg.shape, 1).astype(F32)
    far = float(_R_LANES)
    rmax = lambda x: jnp.max(x, axis=-1, keepdims=True)
    rmin = lambda x: jnp.min(x, axis=-1, keepdims=True)
    rsum = lambda x: jnp.sum(x, axis=-1, keepdims=True)

    gl = jnp.where(lane < N_GROUPS, lg, -jnp.inf)
    gmax = rmax(gl)
    ge = jnp.exp(gl - gmax)
    gp = ge / rsum(ge)
    g_idx = rmin(jnp.where(gl == gmax, lane, far))
    g_w = rsum(jnp.where(lane == g_idx, gp, 0.0))

    lane0 = _EXPERT_LANE0 + g_idx * EXPERTS_PER_GROUP
    smask = (lane >= lane0) & (lane < lane0 + EXPERTS_PER_GROUP)
    sl = jnp.where(smask, lg, -jnp.inf)
    se = jnp.exp(sl - rmax(sl))
    sp = jnp.where(smask, se / rsum(se), -1.0)
    v1 = rmax(sp)
    i1 = rmin(jnp.where(sp == v1, lane, far))
    sp2 = jnp.where(lane == i1, -1.0, sp)
    v2 = rmax(sp2)
    i2 = rmin(jnp.where(sp2 == v2, lane, far))
    den = v1 + v2
    w1 = g_w * (v1 / den)
    w2 = g_w * (v2 / den)

    oh1 = jnp.where(lane == i1, 1.0, 0.0)
    oh2 = jnp.where(lane == i2, 1.0, 0.0)
    r = lax.broadcasted_iota(jnp.int32, (TM_WIDE, TM_WIDE), 0)
    c = lax.broadcasted_iota(jnp.int32, (TM_WIDE, TM_WIDE), 1)
    tri = jnp.where(c < r, 1.0, 0.0).astype(BF16)
    pre1 = jnp.dot(tri, oh1.astype(BF16), preferred_element_type=F32)
    pre2 = jnp.dot(tri, oh2.astype(BF16), preferred_element_type=F32)
    c1 = jnp.sum(oh1, axis=0, keepdims=True)
    c2 = jnp.sum(oh2, axis=0, keepdims=True)
    base = cnt_s[...]
    rank1 = rsum(oh1 * (pre1 + base))
    rank2 = rsum(oh2 * (pre2 + (base + c1)))
    total = base + (c1 + c2)
    cnt_s[...] = total
    cnt_ref[...] = total

    info = jnp.zeros_like(lg)
    for k, val in ((_R_E1, i1 - _EXPERT_LANE0), (_R_E2, i2 - _EXPERT_LANE0), (_R_W1, w1), (_R_W2, w2),
                   (_R_RANK1, rank1), (_R_RANK2, rank2)):
        info = jnp.where(lane == float(k), val, info)
    info_ref[...] = info


def _router(h, g, w_hi, w_lo, bias):
    t_all = h.shape[0]
    const = lambda i: (0, 0)
    return pl.pallas_call(
        _router_kernel,
        grid=(t_all // TM_WIDE,),
        in_specs=[
            pl.BlockSpec((TM_WIDE, D_MODEL), lambda i: (i, 0)),
            pl.BlockSpec((1, D_MODEL), const),
            pl.BlockSpec((D_MODEL, _R_LANES), const),
            pl.BlockSpec((D_MODEL, _R_LANES), const),
            pl.BlockSpec((1, _R_LANES), const),
        ],
        out_specs=[
            pl.BlockSpec((TM_WIDE, D_MODEL // 2), lambda i: (i, 0)),
            pl.BlockSpec((TM_WIDE, _R_LANES), lambda i: (i, 0)),
            pl.BlockSpec((1, _R_LANES), const),
        ],
        out_shape=[
            jax.ShapeDtypeStruct((t_all, D_MODEL // 2), jnp.uint32),
            jax.ShapeDtypeStruct((t_all, _R_LANES), F32),
            jax.ShapeDtypeStruct((1, _R_LANES), F32),
        ],
        scratch_shapes=[pltpu.VMEM((1, _R_LANES), F32)],
        compiler_params=_cparams(("arbitrary",)),
        name="router",
    )(h, g, w_hi, w_lo, bias)


def _moe_ffn_kernel(be_ref, nv_ref, x_ref, wgu_ref, wd_ref, o_ref, wgu_s, wd_s):
    j = pl.program_id(0)
    valid = j < nv_ref[0]
    new_expert = (j == 0) | (be_ref[j] != be_ref[jnp.maximum(j - 1, 0)])

    @pl.when(valid & new_expert)
    def _():
        wgu_s[...] = wgu_ref[...].astype(BF16)
        wd_s[...] = wd_ref[...].astype(BF16)

    @pl.when(valid)
    def _():
        words = x_ref[...]
        x = jnp.concatenate([pltpu.bitcast(words << 16, F32),
                             pltpu.bitcast(words & jnp.uint32(0xFFFF0000), F32)], axis=1).astype(BF16)
        gu = jnp.dot(x, wgu_s[...], preferred_element_type=F32)
        act = jax.nn.silu(gu[:, :D_EXPERT]) * gu[:, D_EXPERT:]
        o_ref[...] = jnp.dot(act.astype(BF16), wd_s[...], preferred_element_type=F32)

    @pl.when(jnp.logical_not(valid))
    def _():
        o_ref[...] = jnp.zeros_like(o_ref)


def _moe_ffn(blk_e, n_valid, xs, wgu, wd, layer):
    n_rows = xs.shape[0]
    n_blk = n_rows // MOE_TM
    return pl.pallas_call(
        _moe_ffn_kernel,
        grid_spec=pltpu.PrefetchScalarGridSpec(
            num_scalar_prefetch=2,
            grid=(n_blk,),
            in_specs=[
                pl.BlockSpec((MOE_TM, D_MODEL // 2), lambda j, be, nv: (j, 0)),
                pl.BlockSpec((None, None, D_MODEL, 2 * D_EXPERT), lambda j, be, nv: (layer, be[j], 0, 0)),
                pl.BlockSpec((None, None, D_EXPERT, D_MODEL), lambda j, be, nv: (layer, be[j], 0, 0)),
            ],
            out_specs=pl.BlockSpec((MOE_TM, D_MODEL), lambda j, be, nv: (j, 0)),
            scratch_shapes=[pltpu.VMEM((D_MODEL, 2 * D_EXPERT), BF16), pltpu.VMEM((D_EXPERT, D_MODEL), BF16)],
        ),
        out_shape=jax.ShapeDtypeStruct((n_rows, D_MODEL), F32),
        compiler_params=_cparams(("arbitrary",)),
        name="moe_ffn",
    )(blk_e, n_valid, xs, wgu, wd)


def _ple_kernel(h_ref, ma_ref, mb_ref, info_ref, g_ref, wg_ref, pp_ref, ps_ref, wp_ref, *o_refs, split):
    i = pl.program_id(0)
    info = info_ref[...]
    w1 = info[:, _R_W1:_R_W1 + 1]
    w2 = info[:, _R_W2:_R_W2 + 1]
    x = h_ref[...] + (w1 * ma_ref[...] + w2 * mb_ref[...])
    hn = _rms(x, g_ref[...]).astype(BF16)
    gate = jax.nn.sigmoid(jnp.dot(hn, wg_ref[...], preferred_element_type=F32))
    p = jnp.where(i == 0, ps_ref[...], pp_ref[...]).astype(BF16)
    out = x + gate * jnp.dot(p, wp_ref[...], preferred_element_type=F32)
    if split:
        yp_ref, ys_ref = o_refs

        @pl.when(i == 0)
        def _():
            ys_ref[...] = out

        @pl.when(i > 0)
        def _():
            yp_ref[...] = out
    else:
        o_refs[0][...] = out


def _ple(h, ma, mb, info, g, wg, p_p, p_s, wp, layer, n_p, split):
    t_all = h.shape[0]
    tile = lambda i: (_tile(i, n_p), 0)
    const = lambda i: (0, 0)
    if split:
        out_specs = [pl.BlockSpec((TM, D_MODEL), lambda i: (_ptile(i), 0)),
                     pl.BlockSpec((TM, D_MODEL), const)]
        out_shape = [jax.ShapeDtypeStruct((n_p * TM, D_MODEL), F32),
                     jax.ShapeDtypeStruct((TM, D_MODEL), F32)]
    else:
        out_specs = pl.BlockSpec((TM, D_MODEL), tile)
        out_shape = jax.ShapeDtypeStruct((t_all, D_MODEL), F32)
    return pl.pallas_call(
        functools.partial(_ple_kernel, split=split),
        grid=(n_p + 1,),
        in_specs=[
            pl.BlockSpec((TM, D_MODEL), tile),
            pl.BlockSpec((TM, D_MODEL), tile),
            pl.BlockSpec((TM, D_MODEL), tile),
            pl.BlockSpec((TM, _R_LANES), tile),
            pl.BlockSpec((1, D_MODEL), const),
            pl.BlockSpec((D_MODEL, D_MODEL), const),
            pl.BlockSpec((None, TM, PLE_DIM), lambda i: (layer, _ptile(i), 0)),
            pl.BlockSpec((None, TM, PLE_DIM), lambda i: (layer, 0, 0)),
            pl.BlockSpec((PLE_DIM, D_MODEL), const),
        ],
        out_specs=out_specs,
        out_shape=out_shape,
        compiler_params=_cparams(("arbitrary",)),
        name="ple",
    )(h, ma, mb, info, g, wg, p_p, p_s, wp)


def _lambda(lp_ref, lambda_init):
    lp = lp_ref[...]
    s1 = jnp.sum(lp[0:1] * lp[1:2], axis=-1, keepdims=True)
    s2 = jnp.sum(lp[2:3] * lp[3:4], axis=-1, keepdims=True)
    return jnp.exp(s1) - jnp.exp(s2) + lambda_init


def _head_out(o0, o1, lam, sg, lambda_init):
    o = o0 - lam * o1
    return _rms(o, sg) * (1.0 - lambda_init)


def _flash_kernel(qt_ref, k_ref, vt_ref, lp_ref, sg_ref, o_ref, acc_s, *, lambda_init):
    qi = pl.program_id(2)
    qt = qt_ref[...]
    zero = jnp.zeros((QK_DIM, TQ), qt.dtype)
    q2t = jnp.concatenate([jnp.concatenate([qt[:QK_DIM], zero], axis=0),
                           jnp.concatenate([zero, qt[QK_DIM:]], axis=0)], axis=1)
    acc_s[...] = jnp.zeros_like(acc_s)
    n_strip = 2 * TQ // _STRIP
    tiles_per_q = TQ // TK

    def attend(j, carry, diag):
        c0 = pl.multiple_of(j * TK, TK)
        kj = k_ref[pl.ds(c0, TK), :]
        vtj = vt_ref[:, pl.ds(c0, TK)]
        out = []
        for c in range(n_strip):
            cs = slice(c * _STRIP, (c + 1) * _STRIP)
            tok0 = (c * _STRIP) % TQ
            key0 = 0 if diag is None else diag * TK
            if diag is not None and key0 > tok0 + _STRIP - 1:
                out.append(carry[c])
                continue
            m_prev, l_prev = carry[c]
            st = jnp.dot(kj, q2t[:, cs], preferred_element_type=F32)
            if diag is not None and key0 + TK - 1 > tok0:
                key = lax.broadcasted_iota(jnp.int32, st.shape, 0) + key0
                tok = lax.broadcasted_iota(jnp.int32, st.shape, 1) + tok0
                st = jnp.where(key <= tok, st, -jnp.inf)
            m_new = jnp.maximum(m_prev, jnp.max(st, axis=0, keepdims=True))
            alpha = jnp.exp2(m_prev - m_new)
            p = jnp.exp2(st - m_new)
            l_new = alpha * l_prev + jnp.sum(p, axis=0, keepdims=True)
            pv = jnp.dot(vtj, p.astype(BF16), preferred_element_type=F32)
            acc_s[:, cs] = alpha * acc_s[:, cs] + pv
            out.append((m_new, l_new))
        return tuple(out)

    def group(jj, carry):
        for u in range(tiles_per_q):
            carry = attend(tiles_per_q * jj + u, carry, None)
        return carry

    init = tuple((jnp.full((1, _STRIP), -jnp.inf, F32), jnp.zeros((1, _STRIP), F32)) for _ in range(n_strip))
    carry = lax.fori_loop(0, qi, group, init)
    for d in range(tiles_per_q):
        carry = attend(tiles_per_q * qi + d, carry, d)

    on = acc_s[...] * (1.0 / jnp.concatenate([l for _, l in carry], axis=1))
    lam = _lambda(lp_ref, lambda_init)
    ot = on[:, :TQ] - lam * on[:, TQ:]
    y = ot * lax.rsqrt(jnp.mean(ot * ot, axis=0, keepdims=True) + EPS) * sg_ref[...]
    o_ref[...] = (y * (1.0 - lambda_init)).T


def _flash(qt_b, k_b, vt_b, lam_p, sub_g_b, batch, seq, lambda_init):
    n_q = seq // TQ
    return pl.pallas_call(
        functools.partial(_flash_kernel, lambda_init=lambda_init),
        grid=(batch, N_HEADS, n_q),
        in_specs=[
            pl.BlockSpec((None, V_DIM, TQ), lambda b, h, i: (b, h, i)),
            pl.BlockSpec((seq, V_DIM), lambda b, h, i: (b, h)),
            pl.BlockSpec((None, V_DIM, seq), lambda b, h, i: (b, h, 0)),
            pl.BlockSpec((4, QK_DIM), lambda b, h, i: (0, 0)),
            pl.BlockSpec((V_DIM, TQ), lambda b, h, i: (0, 0)),
        ],
        out_specs=pl.BlockSpec((TQ, V_DIM), lambda b, h, i: (b * n_q + i, h)),
        out_shape=jax.ShapeDtypeStruct((batch * seq, D_MODEL), F32),
        scratch_shapes=[pltpu.VMEM((V_DIM, 2 * TQ), F32)],
        compiler_params=_cparams(("arbitrary", "arbitrary", "arbitrary")),
        name="flash",
    )(qt_b, k_b, vt_b, lam_p, sub_g_b)


def _decode_kernel(pt_ref, q_ref, ktn_ref, vn_ref, lp_ref, sg_ref, *rest, lambda_init, n_new):
    del pt_ref
    npg = PAGES_PER_STEP
    k_refs = rest[:npg]
    v_refs = rest[npg:2 * npg]
    o_ref = rest[2 * npg]
    q2_s, m_s, l_s, acc_s = rest[2 * npg + 1:]
    p = pl.program_id(1)

    @pl.when(p == 0)
    def _():
        qq = jnp.concatenate([q_ref[...]] * (2 * N_HEADS), axis=0)
        row = lax.broadcasted_iota(jnp.int32, qq.shape, 0)
        lane = lax.broadcasted_iota(jnp.int32, qq.shape, 1)
        q2_s[...] = jnp.where(lane // QK_DIM == row // n_new, qq, 0.0).astype(BF16)
        m_s[...] = jnp.full_like(m_s, -jnp.inf)
        l_s[...] = jnp.zeros_like(l_s)
        acc_s[...] = jnp.zeros_like(acc_s)

    def update(s, v):
        m_prev = m_s[...]
        m_new = jnp.maximum(m_prev, jnp.max(s, axis=-1, keepdims=True))
        alpha = jnp.exp2(m_prev - m_new)
        pr = jnp.exp2(s - m_new)
        l_s[...] = alpha * l_s[...] + jnp.sum(pr, axis=-1, keepdims=True)
        acc_s[...] = alpha * acc_s[...] + jnp.dot(pr.astype(BF16), v, preferred_element_type=F32)
        m_s[...] = m_new

    q2 = q2_s[...]
    s = jnp.concatenate(
        [jnp.dot(q2, k_refs[i][...].astype(BF16), preferred_element_type=F32) for i in range(npg)], axis=1)
    v = jnp.concatenate(
        [jnp.concatenate([v_refs[i][pl.ds(h, PAGE_SIZE, stride=N_HEADS), :] for h in range(N_HEADS)], axis=1)
         for i in range(npg)], axis=0).astype(BF16)
    update(s, v)

    @pl.when(p == pl.num_programs(1) - 1)
    def _():
        sn = jnp.dot(q2, ktn_ref[...].astype(BF16), preferred_element_type=F32)
        row = lax.broadcasted_iota(jnp.int32, sn.shape, 0)
        col = lax.broadcasted_iota(jnp.int32, sn.shape, 1)
        sn = jnp.where(col <= row % n_new, sn, -jnp.inf)
        vn = jnp.concatenate([vn_ref[...], jnp.zeros((PAGE_SIZE - n_new, D_MODEL), F32)], axis=0).astype(BF16)
        update(sn, vn)
        lam = _lambda(lp_ref, lambda_init)
        on = acc_s[...] / l_s[...]
        for h in range(N_HEADS):
            blk = on[h * 2 * n_new:(h + 1) * 2 * n_new, h * V_DIM:(h + 1) * V_DIM]
            o_ref[:, h * V_DIM:(h + 1) * V_DIM] = _head_out(blk[:n_new], blk[n_new:], lam, sg_ref[...],
                                                             lambda_init)


def _decode(pt_flat, q, ktn, v_s, lam_p, sub_g, ck_t, cv_r, layer, n_req, n_new, n_pages, lambda_init):
    npg = PAGES_PER_STEP
    n_steps = n_pages // npg
    n_cols = 2 * N_HEADS * n_new

    def page_spec(i):
        return pl.BlockSpec((None, None, D_MODEL, PAGE_SIZE),
                            lambda b, p, pt: (layer, pt[b * n_pages + p * npg + i], 0, 0))

    return pl.pallas_call(
        functools.partial(_decode_kernel, lambda_init=lambda_init, n_new=n_new),
        grid_spec=pltpu.PrefetchScalarGridSpec(
            num_scalar_prefetch=1,
            grid=(n_req, n_steps),
            in_specs=[
                pl.BlockSpec((n_new, D_MODEL), lambda b, p, pt: (b, 0)),
                pl.BlockSpec((None, D_MODEL, PAGE_SIZE), lambda b, p, pt: (b, 0, 0)),
                pl.BlockSpec((n_new, D_MODEL), lambda b, p, pt: (b, 0)),
                pl.BlockSpec((4, QK_DIM), lambda b, p, pt: (0, 0)),
                pl.BlockSpec((1, V_DIM), lambda b, p, pt: (0, 0)),
            ] + [page_spec(i) for i in range(npg)] + [page_spec(i) for i in range(npg)],
            out_specs=pl.BlockSpec((n_new, D_MODEL), lambda b, p, pt: (b, 0)),
            scratch_shapes=[pltpu.VMEM((n_cols, D_MODEL), BF16), pltpu.VMEM((n_cols, 1), F32),
                            pltpu.VMEM((n_cols, 1), F32), pltpu.VMEM((n_cols, D_MODEL), F32)],
        ),
        out_shape=jax.ShapeDtypeStruct((n_req * n_new, D_MODEL), F32),
        compiler_params=_cparams(("arbitrary", "arbitrary")),
        name="decode_attn",
    )(pt_flat, q, ktn, v_s, lam_p, sub_g, *([ck_t] * npg), *([cv_r] * npg))


def _lru_kernel(u_ref, buf_ref, h0_ref, cw_ref, cb_ref, wa_ref, ba_ref, wi_ref, bi_ref, lam_ref,
                y_ref, hl_ref, ct_ref, xbuf, a_s, b_s, hc, *, ts):
    t = pl.program_id(1)

    @pl.when(t == 0)
    def _():
        xbuf[0:8, :] = buf_ref[...]
        hc[...] = jnp.broadcast_to(h0_ref[...], (8, D_RNN))

    xr = u_ref[:, D_RNN:]
    xbuf[8:8 + ts, :] = xr
    cw = cw_ref[...]
    xc = cb_ref[...] + (cw[0:1] * xbuf[5:5 + ts, :] + cw[1:2] * xbuf[6:6 + ts, :]
                        + cw[2:3] * xbuf[7:7 + ts, :] + cw[3:4] * xr)
    z = -lam_ref[...]
    sp = jnp.maximum(z, 0.0) + jnp.log1p(jnp.exp(-jnp.abs(z)))
    row = lax.broadcasted_iota(jnp.int32, (ts, RG_BLOCK), 0) % 8
    for n in range(N_RG_BLOCKS):
        sl = slice(n * RG_BLOCK, (n + 1) * RG_BLOCK)
        xcn = xc[:, sl]
        xb = xcn.astype(BF16)
        r = jax.nn.sigmoid(jnp.dot(xb, wa_ref[n], preferred_element_type=F32) + ba_ref[:, sl])
        ig = jax.nn.sigmoid(jnp.dot(xb, wi_ref[n], preferred_element_type=F32) + bi_ref[:, sl])
        log_a = -RG_C * r * sp[:, sl]
        a = jnp.exp(log_a)
        mult = jnp.sqrt(-jnp.tanh(log_a) * (a * a + 1.0))
        b = mult * (ig * xcn)
        for s in (1, 2, 4):
            a_sh = pltpu.roll(a, s, 0)
            b_sh = pltpu.roll(b, s, 0)
            keep = row >= s
            b = jnp.where(keep, a * b_sh + b, b)
            a = jnp.where(keep, a * a_sh, a)
        a_s[:, sl] = a
        b_s[:, sl] = b

    def step(i, h):
        r0 = pl.multiple_of(i * 8, 8)
        ht = a_s[pl.ds(r0, 8), :] * h + b_s[pl.ds(r0, 8), :]
        b_s[pl.ds(r0, 8), :] = ht
        return jnp.broadcast_to(ht[7:8, :], (8, D_RNN))

    h = lax.fori_loop(0, ts // 8, step, hc[...])
    hc[...] = h
    xbuf[0:8, :] = xbuf[ts:ts + 8, :]
    y_ref[...] = b_s[...] * jax.nn.gelu(u_ref[:, :D_RNN])

    @pl.when(t == pl.num_programs(1) - 1)
    def _():
        hl_ref[...] = h[0:1]
        ct_ref[...] = xbuf[0:8, :]


def _lru(u, buf0, h0, cw, cb, wa, ba, wi, bi, lam, n_seq, seq, ts, row_blk0):
    nt = seq // ts
    c2 = lambda b, t: (0, 0)
    c3 = lambda b, t: (0, 0, 0)
    return pl.pallas_call(
        functools.partial(_lru_kernel, ts=ts),
        grid=(n_seq, nt),
        in_specs=[
            pl.BlockSpec((ts, 2 * D_RNN), lambda b, t: (row_blk0 + b * nt + t, 0)),
            pl.BlockSpec((None, 8, D_RNN), lambda b, t: (b, 0, 0)),
            pl.BlockSpec((None, 1, D_RNN), lambda b, t: (b, 0, 0)),
            pl.BlockSpec((CONV_W, D_RNN), c2),
            pl.BlockSpec((1, D_RNN), c2),
            pl.BlockSpec((N_RG_BLOCKS, RG_BLOCK, RG_BLOCK), c3),
            pl.BlockSpec((1, D_RNN), c2),
            pl.BlockSpec((N_RG_BLOCKS, RG_BLOCK, RG_BLOCK), c3),
            pl.BlockSpec((1, D_RNN), c2),
            pl.BlockSpec((1, D_RNN), c2),
        ],
        out_specs=[
            pl.BlockSpec((ts, D_RNN), lambda b, t: (b * nt + t, 0)),
            pl.BlockSpec((None, 1, D_RNN), lambda b, t: (b, 0, 0)),
            pl.BlockSpec((None, 8, D_RNN), lambda b, t: (b, 0, 0)),
        ],
        out_shape=[
            jax.ShapeDtypeStruct((n_seq * seq, D_RNN), F32),
            jax.ShapeDtypeStruct((n_seq, 1, D_RNN), F32),
            jax.ShapeDtypeStruct((n_seq, 8, D_RNN), F32),
        ],
        scratch_shapes=[pltpu.VMEM((ts + 8, D_RNN), F32), pltpu.VMEM((ts, D_RNN), F32),
                        pltpu.VMEM((ts, D_RNN), F32), pltpu.VMEM((8, D_RNN), F32)],
        compiler_params=_cparams(("arbitrary", "arbitrary")),
        name="lru",
    )(u, buf0, h0, cw, cb, wa, ba, wi, bi, lam)


def _dispatch(info, counts_row):
    t_all = info.shape[0]
    eid = info[:, _R_E1:_R_E2 + 1].astype(jnp.int32)
    rank = info[:, _R_RANK1:_R_RANK2 + 1].astype(jnp.int32)
    counts = counts_row[0, _EXPERT_LANE0:_EXPERT_LANE0 + N_EXPERTS].astype(jnp.int32)
    pcounts = ((counts + MOE_TM - 1) // MOE_TM) * MOE_TM
    pends = jnp.cumsum(pcounts)
    pstarts = pends - pcounts
    dest = pstarts.at[eid].get(mode='promise_in_bounds') + rank
    n_blk = (t_all * TOP_K) // MOE_TM + N_EXPERTS
    blk_row0 = jnp.arange(n_blk, dtype=jnp.int32) * MOE_TM
    blk_e = jnp.minimum(jnp.sum((pends[None, :] <= blk_row0[:, None]).astype(jnp.int32), axis=1), N_EXPERTS - 1)
    n_valid = (pends[-1] // MOE_TM).astype(jnp.int32).reshape(1)
    tok = jnp.broadcast_to(jnp.arange(t_all, dtype=jnp.int32)[:, None], (t_all, TOP_K))
    row_tok = (jnp.arange(n_blk * MOE_TM, dtype=jnp.int32) % t_all).at[dest.reshape(-1)].set(
        tok.reshape(-1), mode='promise_in_bounds', unique_indices=True)
    return dest, blk_e.astype(jnp.int32), n_valid, row_tok


def kernel(x_prompt, x_sample, cache_k, cache_v, state_h, state_conv, page_table, p_prompt, p_sample, mix_norm, ffn_norm, ple_norm, w_qkv, q_norm, k_norm, lambda_q1, lambda_k1, lambda_q2, lambda_k2, sub_norm, w_o, w_in_lru, conv_w, conv_b, w_rg_a, b_rg_a, w_rg_i, b_rg_i, lru_lambda, w_out_lru, w_group, b_group, w_sub, b_sub, w_gate_up, w_down, w_ple_gate, w_ple_proj):
    batch, seq, _ = x_prompt.shape
    n_req, n_new, _ = x_sample.shape
    n_pages = page_table.shape[1]
    past_len = n_pages * PAGE_SIZE
    t_p = batch * seq
    t_s = n_req * n_new
    assert t_s == TM and t_p % TM == 0 and seq % TM == 0 and n_pages % PAGES_PER_STEP == 0
    assert (t_p + t_s) % TM_WIDE == 0 and seq % TQ == 0
    n_p = t_p // TM
    n_attn = cache_k.shape[0]
    n_pool = cache_k.shape[1]

    h = jnp.concatenate([x_prompt.reshape(t_p, D_MODEL), x_sample.reshape(t_s, D_MODEL)], axis=0)

    pos = jnp.concatenate([jnp.tile(jnp.arange(seq, dtype=jnp.int32), batch),
                           jnp.tile(past_len + jnp.arange(n_new, dtype=jnp.int32), n_req)])
    half = ROPE_DIM // 2
    inv = ROPE_THETA ** (-(jnp.arange(half, dtype=F32) * 2.0 / ROPE_DIM))
    ang = pos.astype(F32)[:, None] * inv[None, :]
    cos_t = jnp.cos(ang).T
    sin_t = jnp.sin(ang).T

    ck_t = cache_k.transpose(0, 1, 3, 4, 5, 2).reshape(n_attn, n_pool, D_MODEL, PAGE_SIZE)
    cv_r = cache_v.reshape(n_attn, n_pool, PAGE_SIZE * N_HEADS, V_DIM)
    pt_flat = page_table.reshape(-1).astype(jnp.int32)
    pp = p_prompt.reshape(DEPTH, t_p, PLE_DIM)
    ps = p_sample.reshape(DEPTH, t_s, PLE_DIM)

    k_s, v_s, hl_p, ct_p, hl_s, ct_s = [], [], [], [], [], []
    y_p = y_s = kv_prev = None
    for i in range(DEPTH):
        j = i // 2
        g_mix = mix_norm[i].reshape(1, D_MODEL).astype(F32)
        if i % 2 == 0:
            lambda_init = 0.8 - 0.6 * math.exp(-0.3 * i)
            wq_t = w_qkv[j, :, :D_MODEL].T.astype(BF16)
            wk_t = w_qkv[j, :, D_MODEL:2 * D_MODEL].T.astype(BF16)
            wv = w_qkv[j, :, 2 * D_MODEL:].astype(BF16)
            qg_b = jnp.broadcast_to(q_norm[j].astype(F32)[:, None], (QK_DIM, TM))
            kg_b = jnp.broadcast_to(k_norm[j].astype(F32)[:, None], (QK_DIM, TM))
            qt_b, q_s, kt_p, kt_s, k_b, vv_p, vv_s, vt_b = _qkv(h, g_mix, wq_t, wk_t, wv, qg_b, kg_b, cos_t, sin_t,
                                                                n_p, batch, seq, j, n_attn, kv_prev)
            kv_prev = (kt_p, vv_p)
            lam_p = jnp.stack([lambda_q1[j], lambda_k1[j], lambda_q2[j], lambda_k2[j]]).astype(F32)
            sub_g = sub_norm[j].reshape(1, V_DIM).astype(F32)
            sub_g_b = jnp.broadcast_to(sub_norm[j].astype(F32)[:, None], (V_DIM, TQ))
            o_p = _flash(qt_b, k_b, vt_b, lam_p, sub_g_b, batch, seq, lambda_init)
            ktn = kt_s.reshape(D_MODEL, n_req, n_new).transpose(1, 0, 2)
            ktn = jnp.pad(ktn, ((0, 0), (0, 0), (0, PAGE_SIZE - n_new)))
            o_s = _decode(pt_flat, q_s, ktn, vv_s, lam_p, sub_g, ck_t, cv_r, j, n_req, n_new, n_pages,
                          lambda_init)
            h = _resid_matmul(h, o_p, o_s, w_o[j].astype(BF16), n_p)
            k_s.append(kt_s.T.reshape(n_req, n_new, N_HEADS, 2, QK_DIM))
            v_s.append(vv_s.reshape(n_req, n_new, N_HEADS, V_DIM))
        else:
            u = _norm_matmul(h, g_mix, w_in_lru[j].astype(BF16))
            cw = conv_w[j].astype(F32)
            cb = conv_b[j].reshape(1, D_RNN).astype(F32)
            wa = w_rg_a[j].astype(BF16)
            wi = w_rg_i[j].astype(BF16)
            ba = b_rg_a[j].reshape(1, D_RNN).astype(F32)
            bi = b_rg_i[j].reshape(1, D_RNN).astype(F32)
            lam = lru_lambda[j].reshape(1, D_RNN).astype(F32)
            zb = jnp.zeros((batch, 8, D_RNN), F32)
            zh = jnp.zeros((batch, 1, D_RNN), F32)
            yl_p, hlp, ctp = _lru(u, zb, zh, cw, cb, wa, ba, wi, bi, lam, batch, seq, LRU_TS, 0)
            buf_s = jnp.pad(state_conv[j].astype(F32), ((0, 0), (8 - (CONV_W - 1), 0), (0, 0)))
            h0_s = state_h[j].astype(F32)[:, None, :]
            yl_s, hls, cts = _lru(u, buf_s, h0_s, cw, cb, wa, ba, wi, bi, lam, n_req, n_new, n_new,
                                  t_p // n_new)
            h = _resid_matmul(h, yl_p, yl_s, w_out_lru[j].astype(BF16), n_p)
            hl_p.append(hlp[:, 0])
            ct_p.append(ctp[:, 8 - (CONV_W - 1):])
            hl_s.append(hls[:, 0])
            ct_s.append(cts[:, 8 - (CONV_W - 1):])

        w_r = jnp.concatenate([w_group[i], w_sub[i]], axis=1).astype(F32)
        w_r = jnp.pad(w_r, ((0, 0), (0, 128 - w_r.shape[1])))
        w_hi = w_r.astype(BF16)
        w_lo = (w_r - w_hi.astype(F32)).astype(BF16)
        bias = jnp.concatenate([b_group[i], b_sub[i]]).astype(F32)
        bias = jnp.pad(bias, (0, _R_LANES - bias.shape[0])).reshape(1, _R_LANES)
        hn, info, counts = _router(h, ffn_norm[i].reshape(1, D_MODEL).astype(F32), w_hi, w_lo, bias)
        dest, blk_e, n_valid, row_tok = _dispatch(info, counts)
        xs = hn.at[row_tok].get(mode='promise_in_bounds')
        ys = _moe_ffn(blk_e, n_valid, xs, w_gate_up, w_down, i)
        ma = ys.at[dest[:, 0]].get(mode='promise_in_bounds')
        mb = ys.at[dest[:, 1]].get(mode='promise_in_bounds')

        last = i == DEPTH - 1
        out = _ple(h, ma, mb, info, ple_norm[i].reshape(1, D_MODEL).astype(F32), w_ple_gate[i].astype(BF16),
                   pp, ps, w_ple_proj[i].astype(BF16), i, n_p, last)
        if last:
            y_p, y_s = out
        else:
            h = out

    kt_all, v_all = kv_prev
    k_prompt = kt_all.reshape(n_attn, batch, N_HEADS, 2, QK_DIM, seq).transpose(0, 1, 5, 2, 3, 4)
    v_prompt = v_all.reshape(n_attn, batch, seq, N_HEADS, V_DIM)
    return (y_p.reshape(batch, seq, D_MODEL), y_s.reshape(n_req, n_new, D_MODEL),
            k_prompt, v_prompt, jnp.stack(hl_p), jnp.stack(ct_p),
            jnp.stack(k_s), jnp.stack(v_s), jnp.stack(hl_s), jnp.stack(ct_s))
```

```python
import functools
import math

import jax
import jax.numpy as jnp
import numpy as np
from jax import lax
from jax.experimental import pallas as pl
from jax.experimental.pallas import tpu as pltpu

F32 = jnp.float32
BF16 = jnp.bfloat16

D_MODEL = 1024
DEPTH = 4
PAGE_SIZE = 128
N_HEADS = 8
QK_DIM = 64
V_DIM = 128
ROPE_DIM = 16
ROPE_THETA = 500000.0
D_RNN = 1280
RG_BLOCK = 128
N_RG_BLOCKS = 10
CONV_W = 4
RG_C = 8.0
N_GROUPS = 4
EXPERTS_PER_GROUP = 8
N_EXPERTS = 32
TOP_K = 2
D_EXPERT = 512
PLE_DIM = 256
EPS = 1e-6

TM = 256
TM_WIDE = 640
TQ = 512
TK = 256
PAGES_PER_STEP = 16
MOE_TM = 512
LRU_TS = 256
VMEM_LIMIT = 56 * 1024 * 1024

_NT = (((1,), (1,)), ((), ()))
_STRIP = 128
_Q_SCALE = QK_DIM ** -0.5 * math.log2(math.e)


def _cparams(sem):
    return pltpu.CompilerParams(dimension_semantics=sem, vmem_limit_bytes=VMEM_LIMIT)


def _rms(x, g):
    return x * lax.rsqrt(jnp.mean(x * x, axis=-1, keepdims=True) + EPS) * g


def _tile(i, n_p):
    return jnp.where(i == 0, n_p, i - 1)


def _ptile(i):
    return jnp.maximum(i - 1, 0)


def _qkv_kernel(h_ref, g_ref, wq_ref, wk_ref, wv_ref, qg_ref, kg_ref, cos_ref, sin_ref, *rest):
    qtb_ref, qs_ref, ktp_ref, kts_ref, kb_ref, vp_ref, vs_ref, vtb_ref, qt_s, kt_s = rest[-10:]
    i = pl.program_id(0)
    hn = _rms(h_ref[...], g_ref[...]).astype(BF16)
    cos = cos_ref[...]
    sin = sin_ref[...]

    def normrope(w_ref, gain_ref, dst, scale):
        xt = lax.dot_general(w_ref[...], hn, _NT, preferred_element_type=F32)
        gain = gain_ref[...]
        for g in range(2 * N_HEADS):
            xg = xt[g * QK_DIM:(g + 1) * QK_DIM, :]
            ss = jnp.sum(xg * xg, axis=0, keepdims=True)
            yg = xg * lax.rsqrt(ss * (1.0 / QK_DIM) + EPS) * gain
            y1 = yg[0:8]
            y2 = yg[8:16]
            dst[g * QK_DIM:g * QK_DIM + 8, :] = (y1 * cos - y2 * sin) * scale
            dst[g * QK_DIM + 8:g * QK_DIM + 16, :] = (y2 * cos + y1 * sin) * scale
            dst[g * QK_DIM + 16:(g + 1) * QK_DIM, :] = yg[16:] * scale

    normrope(wq_ref, qg_ref, qt_s, _Q_SCALE)
    normrope(wk_ref, kg_ref, kt_s, 1.0)
    v = jnp.dot(hn, wv_ref[...], preferred_element_type=F32)

    @pl.when(i == 0)
    def _():
        qs_ref[...] = qt_s[...].T
        kts_ref[...] = kt_s[...]
        vs_ref[...] = v

    @pl.when(i > 0)
    def _():
        kt = kt_s[...]
        qtb_ref[...] = qt_s[...].astype(BF16)
        ktp_ref[...] = kt
        kb_ref[...] = kt.T.astype(BF16)
        vp_ref[...] = v
        vtb_ref[...] = v.T.astype(BF16)


def _qkv(h, g, wq_t, wk_t, wv, qg_b, kg_b, cos_t, sin_t, n_p, batch, seq, layer, n_attn, prev):
    n_sb = seq // TM
    const = lambda i: (0, 0)
    tile = lambda i: (_tile(i, n_p), 0)
    ktile = lambda i: (_ptile(i) // n_sb, 0, _ptile(i) % n_sb)
    ltile = lambda i: (layer, _ptile(i) // n_sb, 0, _ptile(i) % n_sb)
    prev = () if prev is None else tuple(prev)
    n_in = 9
    return pl.pallas_call(
        _qkv_kernel,
        grid=(n_p + 1,),
        in_specs=[
            pl.BlockSpec((TM, D_MODEL), tile),
            pl.BlockSpec((1, D_MODEL), const),
            pl.BlockSpec((D_MODEL, D_MODEL), const),
            pl.BlockSpec((D_MODEL, D_MODEL), const),
            pl.BlockSpec((D_MODEL, D_MODEL), const),
            pl.BlockSpec((QK_DIM, TM), const),
            pl.BlockSpec((QK_DIM, TM), const),
            pl.BlockSpec((8, TM), lambda i: (0, _tile(i, n_p))),
            pl.BlockSpec((8, TM), lambda i: (0, _tile(i, n_p))),
        ] + [pl.BlockSpec(memory_space=pl.ANY)] * len(prev),
        out_specs=[
            pl.BlockSpec((None, D_MODEL, TM), ktile),
            pl.BlockSpec((TM, D_MODEL), const),
            pl.BlockSpec((None, None, D_MODEL, TM), ltile),
            pl.BlockSpec((D_MODEL, TM), const),
            pl.BlockSpec((TM, D_MODEL), lambda i: (_ptile(i), 0)),
            pl.BlockSpec((None, TM, D_MODEL), lambda i: (layer, _ptile(i), 0)),
            pl.BlockSpec((TM, D_MODEL), const),
            pl.BlockSpec((None, D_MODEL, TM), ktile),
        ],
        out_shape=[
            jax.ShapeDtypeStruct((batch, D_MODEL, seq), BF16),
            jax.ShapeDtypeStruct((TM, D_MODEL), F32),
            jax.ShapeDtypeStruct((n_attn, batch, D_MODEL, seq), F32),
            jax.ShapeDtypeStruct((D_MODEL, TM), F32),
            jax.ShapeDtypeStruct((n_p * TM, D_MODEL), BF16),
            jax.ShapeDtypeStruct((n_attn, n_p * TM, D_MODEL), F32),
            jax.ShapeDtypeStruct((TM, D_MODEL), F32),
            jax.ShapeDtypeStruct((batch, D_MODEL, seq), BF16),
        ],
        input_output_aliases={n_in: 2, n_in + 1: 5} if prev else {},
        scratch_shapes=[pltpu.VMEM((D_MODEL, TM), F32), pltpu.VMEM((D_MODEL, TM), F32)],
        compiler_params=_cparams(("arbitrary",)),
        name="qkv",
    )(h, g, wq_t, wk_t, wv, qg_b, kg_b, cos_t, sin_t, *prev)


def _resid_matmul_kernel(h_ref, xp_ref, xs_ref, w_ref, o_ref):
    i = pl.program_id(0)
    x = jnp.where(i == 0, xs_ref[...].astype(BF16), xp_ref[...].astype(BF16))
    o_ref[...] = h_ref[...] + jnp.dot(x, w_ref[...], preferred_element_type=F32)


def _resid_matmul(h, x_p, x_s, w, n_p):
    t_all = h.shape[0]
    k = w.shape[0]
    tile = lambda i: (_tile(i, n_p), 0)
    return pl.pallas_call(
        _resid_matmul_kernel,
        grid=(n_p + 1,),
        in_specs=[
            pl.BlockSpec((TM, D_MODEL), tile),
            pl.BlockSpec((TM, k), lambda i: (_ptile(i), 0)),
            pl.BlockSpec((TM, k), lambda i: (0, 0)),
            pl.BlockSpec((k, D_MODEL), lambda i: (0, 0)),
        ],
        out_specs=pl.BlockSpec((TM, D_MODEL), tile),
        out_shape=jax.ShapeDtypeStruct((t_all, D_MODEL), F32),
        compiler_params=_cparams(("arbitrary",)),
        name="resid_matmul",
    )(h, x_p, x_s, w)


def _norm_matmul_kernel(h_ref, g_ref, w_ref, o_ref):
    hn = _rms(h_ref[...], g_ref[...]).astype(BF16)
    o_ref[...] = jnp.dot(hn, w_ref[...], preferred_element_type=F32)


def _norm_matmul(h, g, w):
    t_all = h.shape[0]
    n = w.shape[1]
    return pl.pallas_call(
        _norm_matmul_kernel,
        grid=(t_all // TM_WIDE,),
        in_specs=[
            pl.BlockSpec((TM_WIDE, D_MODEL), lambda i: (i, 0)),
            pl.BlockSpec((1, D_MODEL), lambda i: (0, 0)),
            pl.BlockSpec((D_MODEL, n), lambda i: (0, 0)),
        ],
        out_specs=pl.BlockSpec((TM_WIDE, n), lambda i: (i, 0)),
        out_shape=jax.ShapeDtypeStruct((t_all, n), F32),
        compiler_params=_cparams(("arbitrary",)),
        name="norm_matmul",
    )(h, g, w)


_R_E1, _R_E2, _R_W1, _R_W2, _R_RANK1, _R_RANK2 = range(6)
_R_LANES = 128
_EXPERT_LANE0 = N_GROUPS


def _router_kernel(h_ref, g_ref, whi_ref, wlo_ref, bias_ref, hn_ref, info_ref, cnt_ref, cnt_s):
    i = pl.program_id(0)

    @pl.when(i == 0)
    def _():
        cnt_s[...] = jnp.zeros_like(cnt_s)

    hn = _rms(h_ref[...], g_ref[...])
    hi = hn.astype(BF16)
    bits = pltpu.bitcast(hi.astype(F32), jnp.uint32)
    hn_ref[...] = (bits[:, :D_MODEL // 2] >> 16) | bits[:, D_MODEL // 2:]
    lo = (hn - hi.astype(F32)).astype(BF16)
    whi = whi_ref[...]
    lg = jnp.dot(hi, whi, preferred_element_type=F32)
    lg += jnp.dot(lo, whi, preferred_element_type=F32)
    lg += jnp.dot(hi, wlo_ref[...], preferred_element_type=F32)
    lg += bias_ref[...]

    lane = lax.broadcasted_iota(jnp.int32, lg.shape, 1).astype(F32)
    far = float(_R_LANES)
    rmax = lambda x: jnp.max(x, axis=-1, keepdims=True)
    rmin = lambda x: jnp.min(x, axis=-1, keepdims=True)
    rsum = lambda x: jnp.sum(x, axis=-1, keepdims=True)

    gl = jnp.where(lane < N_GROUPS, lg, -jnp.inf)
    gmax = rmax(gl)
    ge = jnp.exp(gl - gmax)
    gp = ge / rsum(ge)
    g_idx = rmin(jnp.where(gl == gmax, lane, far))
    g_w = rsum(jnp.where(lane == g_idx, gp, 0.0))

    lane0 = _EXPERT_LANE0 + g_idx * EXPERTS_PER_GROUP
    smask = (lane >= lane0) & (lane < lane0 + EXPERTS_PER_GROUP)
    sl = jnp.where(smask, lg, -jnp.inf)
    se = jnp.exp(sl - rmax(sl))
    sp = jnp.where(smask, se / rsum(se), -1.0)
    v1 = rmax(sp)
    i1 = rmin(jnp.where(sp == v1, lane, far))
    sp2 = jnp.where(lane == i1, -1.0, sp)
    v2 = rmax(sp2)
    i2 = rmin(jnp.where(sp2 == v2, lane, far))
    den = v1 + v2
    w1 = g_w * (v1 / den)
    w2 = g_w * (v2 / den)

    oh1 = jnp.where(lane == i1, 1.0, 0.0)
    oh2 = jnp.where(lane == i2, 1.0, 0.0)
    r = lax.broadcasted_iota(jnp.int32, (TM_WIDE, TM_WIDE), 0)
    c = lax.broadcasted_iota(jnp.int32, (TM_WIDE, TM_WIDE), 1)
    tri = jnp.where(c < r, 1.0, 0.0).astype(BF16)
    pre1 = jnp.dot(tri, oh1.astype(BF16), preferred_element_type=F32)
    pre2 = jnp.dot(tri, oh2.astype(BF16), preferred_element_type=F32)
    c1 = jnp.sum(oh1, axis=0, keepdims=True)
    c2 = jnp.sum(oh2, axis=0, keepdims=True)
    base = cnt_s[...]
    rank1 = rsum(oh1 * (pre1 + base))
    rank2 = rsum(oh2 * (pre2 + (base + c1)))
    total = base + (c1 + c2)
    cnt_s[...] = total
    cnt_ref[...] = total

    info = jnp.zeros_like(lg)
    for k, val in ((_R_E1, i1 - _EXPERT_LANE0), (_R_E2, i2 - _EXPERT_LANE0), (_R_W1, w1), (_R_W2, w2),
                   (_R_RANK1, rank1), (_R_RANK2, rank2)):
        info = jnp.where(lane == float(k), val, info)
    info_ref[...] = info


def _router(h, g, w_hi, w_lo, bias):
    t_all = h.shape[0]
    const = lambda i: (0, 0)
    return pl.pallas_call(
        _router_kernel,
        grid=(t_all // TM_WIDE,),
        in_specs=[
            pl.BlockSpec((TM_WIDE, D_MODEL), lambda i: (i, 0)),
            pl.BlockSpec((1, D_MODEL), const),
            pl.BlockSpec((D_MODEL, _R_LANES), const),
            pl.BlockSpec((D_MODEL, _R_LANES), const),
            pl.BlockSpec((1, _R_LANES), const),
        ],
        out_specs=[
            pl.BlockSpec((TM_WIDE, D_MODEL // 2), lambda i: (i, 0)),
            pl.BlockSpec((TM_WIDE, _R_LANES), lambda i: (i, 0)),
            pl.BlockSpec((1, _R_LANES), const),
        ],
        out_shape=[
            jax.ShapeDtypeStruct((t_all, D_MODEL // 2), jnp.uint32),
            jax.ShapeDtypeStruct((t_all, _R_LANES), F32),
            jax.ShapeDtypeStruct((1, _R_LANES), F32),
        ],
        scratch_shapes=[pltpu.VMEM((1, _R_LANES), F32)],
        compiler_params=_cparams(("arbitrary",)),
        name="router",
    )(h, g, w_hi, w_lo, bias)


def _moe_ffn_kernel(be_ref, nv_ref, x_ref, wgu_ref, wd_ref, o_ref, wgu_s, wd_s):
    j = pl.program_id(0)
    valid = j < nv_ref[0]
    new_expert = (j == 0) | (be_ref[j] != be_ref[jnp.maximum(j - 1, 0)])

    @pl.when(valid & new_expert)
    def _():
        wgu_s[...] = wgu_ref[...].astype(BF16)
        wd_s[...] = wd_ref[...].astype(BF16)

    @pl.when(valid)
    def _():
        words = x_ref[...]
        x = jnp.concatenate([pltpu.bitcast(words << 16, F32),
                             pltpu.bitcast(words & jnp.uint32(0xFFFF0000), F32)], axis=1).astype(BF16)
        gu = jnp.dot(x, wgu_s[...], preferred_element_type=F32)
        act = jax.nn.silu(gu[:, :D_EXPERT]) * gu[:, D_EXPERT:]
        o_ref[...] = jnp.dot(act.astype(BF16), wd_s[...], preferred_element_type=F32)

    @pl.when(jnp.logical_not(valid))
    def _():
        o_ref[...] = jnp.zeros_like(o_ref)


def _moe_ffn(blk_e, n_valid, xs, wgu, wd, layer):
    n_rows = xs.shape[0]
    n_blk = n_rows // MOE_TM
    return pl.pallas_call(
        _moe_ffn_kernel,
        grid_spec=pltpu.PrefetchScalarGridSpec(
            num_scalar_prefetch=2,
            grid=(n_blk,),
            in_specs=[
                pl.BlockSpec((MOE_TM, D_MODEL // 2), lambda j, be, nv: (j, 0)),
                pl.BlockSpec((None, None, D_MODEL, 2 * D_EXPERT), lambda j, be, nv: (layer, be[j], 0, 0)),
                pl.BlockSpec((None, None, D_EXPERT, D_MODEL), lambda j, be, nv: (layer, be[j], 0, 0)),
            ],
            out_specs=pl.BlockSpec((MOE_TM, D_MODEL), lambda j, be, nv: (j, 0)),
            scratch_shapes=[pltpu.VMEM((D_MODEL, 2 * D_EXPERT), BF16), pltpu.VMEM((D_EXPERT, D_MODEL), BF16)],
        ),
        out_shape=jax.ShapeDtypeStruct((n_rows, D_MODEL), F32),
        compiler_params=_cparams(("arbitrary",)),
        name="moe_ffn",
    )(blk_e, n_valid, xs, wgu, wd)


def _ple_kernel(h_ref, ma_ref, mb_ref, info_ref, g_ref, wg_ref, pp_ref, ps_ref, wp_ref, *o_refs, split):
    i = pl.program_id(0)
    info = info_ref[...]
    w1 = info[:, _R_W1:_R_W1 + 1]
    w2 = info[:, _R_W2:_R_W2 + 1]
    x = h_ref[...] + (w1 * ma_ref[...] + w2 * mb_ref[...])
    hn = _rms(x, g_ref[...]).astype(BF16)
    gate = jax.nn.sigmoid(jnp.dot(hn, wg_ref[...], preferred_element_type=F32))
    p = jnp.where(i == 0, ps_ref[...], pp_ref[...]).astype(BF16)
    out = x + gate * jnp.dot(p, wp_ref[...], preferred_element_type=F32)
    if split:
        yp_ref, ys_ref = o_refs

        @pl.when(i == 0)
        def _():
            ys_ref[...] = out

        @pl.when(i > 0)
        def _():
            yp_ref[...] = out
    else:
        o_refs[0][...] = out


def _ple(h, ma, mb, info, g, wg, p_p, p_s, wp, layer, n_p, split):
    t_all = h.shape[0]
    tile = lambda i: (_tile(i, n_p), 0)
    const = lambda i: (0, 0)
    if split:
        out_specs = [pl.BlockSpec((TM, D_MODEL), lambda i: (_ptile(i), 0)),
                     pl.BlockSpec((TM, D_MODEL), const)]
        out_shape = [jax.ShapeDtypeStruct((n_p * TM, D_MODEL), F32),
                     jax.ShapeDtypeStruct((TM, D_MODEL), F32)]
    else:
        out_specs = pl.BlockSpec((TM, D_MODEL), tile)
        out_shape = jax.ShapeDtypeStruct((t_all, D_MODEL), F32)
    return pl.pallas_call(
        functools.partial(_ple_kernel, split=split),
        grid=(n_p + 1,),
        in_specs=[
            pl.BlockSpec((TM, D_MODEL), tile),
            pl.BlockSpec((TM, D_MODEL), tile),
            pl.BlockSpec((TM, D_MODEL), tile),
            pl.BlockSpec((TM, _R_LANES), tile),
            pl.BlockSpec((1, D_MODEL), const),
            pl.BlockSpec((D_MODEL, D_MODEL), const),
            pl.BlockSpec((None, TM, PLE_DIM), lambda i: (layer, _ptile(i), 0)),
            pl.BlockSpec((None, TM, PLE_DIM), lambda i: (layer, 0, 0)),
            pl.BlockSpec((PLE_DIM, D_MODEL), const),
        ],
        out_specs=out_specs,
        out_shape=out_shape,
        compiler_params=_cparams(("arbitrary",)),
        name="ple",
    )(h, ma, mb, info, g, wg, p_p, p_s, wp)


def _lambda(lp_ref, lambda_init):
    lp = lp_ref[...]
    s1 = jnp.sum(lp[0:1] * lp[1:2], axis=-1, keepdims=True)
    s2 = jnp.sum(lp[2:3] * lp[3:4], axis=-1, keepdims=True)
    return jnp.exp(s1) - jnp.exp(s2) + lambda_init


def _head_out(o0, o1, lam, sg, lambda_init):
    o = o0 - lam * o1
    return _rms(o, sg) * (1.0 - lambda_init)


def _flash_kernel(qt_ref, k_ref, vt_ref, lp_ref, sg_ref, o_ref, acc_s, *, lambda_init):
    qi = pl.program_id(2)
    qt = qt_ref[...]
    zero = jnp.zeros((QK_DIM, TQ), qt.dtype)
    q2t = jnp.concatenate([jnp.concatenate([qt[:QK_DIM], zero], axis=0),
                           jnp.concatenate([zero, qt[QK_DIM:]], axis=0)], axis=1)
    acc_s[...] = jnp.zeros_like(acc_s)
    n_strip = 2 * TQ // _STRIP
    tiles_per_q = TQ // TK

    def attend(j, carry, diag):
        c0 = pl.multiple_of(j * TK, TK)
        kj = k_ref[pl.ds(c0, TK), :]
        vtj = vt_ref[:, pl.ds(c0, TK)]
        out = []
        for c in range(n_strip):
            cs = slice(c * _STRIP, (c + 1) * _STRIP)
            tok0 = (c * _STRIP) % TQ
            key0 = 0 if diag is None else diag * TK
            if diag is not None and key0 > tok0 + _STRIP - 1:
                out.append(carry[c])
                continue
            m_prev, l_prev = carry[c]
            st = jnp.dot(kj, q2t[:, cs], preferred_element_type=F32)
            if diag is not None and key0 + TK - 1 > tok0:
                key = lax.broadcasted_iota(jnp.int32, st.shape, 0) + key0
                tok = lax.broadcasted_iota(jnp.int32, st.shape, 1) + tok0
                st = jnp.where(key <= tok, st, -jnp.inf)
            m_new = jnp.maximum(m_prev, jnp.max(st, axis=0, keepdims=True))
            alpha = jnp.exp2(m_prev - m_new)
            p = jnp.exp2(st - m_new)
            l_new = alpha * l_prev + jnp.sum(p, axis=0, keepdims=True)
            pv = jnp.dot(vtj, p.astype(BF16), preferred_element_type=F32)
            acc_s[:, cs] = alpha * acc_s[:, cs] + pv
            out.append((m_new, l_new))
        return tuple(out)

    def group(jj, carry):
        for u in range(tiles_per_q):
            carry = attend(tiles_per_q * jj + u, carry, None)
        return carry

    init = tuple((jnp.full((1, _STRIP), -jnp.inf, F32), jnp.zeros((1, _STRIP), F32)) for _ in range(n_strip))
    carry = lax.fori_loop(0, qi, group, init)
    for d in range(tiles_per_q):
        carry = attend(tiles_per_q * qi + d, carry, d)

    on = acc_s[...] * (1.0 / jnp.concatenate([l for _, l in carry], axis=1))
    lam = _lambda(lp_ref, lambda_init)
    ot = on[:, :TQ] - lam * on[:, TQ:]
    y = ot * lax.rsqrt(jnp.mean(ot * ot, axis=0, keepdims=True) + EPS) * sg_ref[...]
    o_ref[...] = (y * (1.0 - lambda_init)).T.astype(o_ref.dtype)


def _flash(qt_b, k_b, vt_b, lam_p, sub_g_b, batch, seq, lambda_init):
    n_q = seq // TQ
    return pl.pallas_call(
        functools.partial(_flash_kernel, lambda_init=lambda_init),
        grid=(batch, N_HEADS, n_q),
        in_specs=[
            pl.BlockSpec((None, V_DIM, TQ), lambda b, h, i: (b, h, i)),
            pl.BlockSpec((seq, V_DIM), lambda b, h, i: (b, h)),
            pl.BlockSpec((None, V_DIM, seq), lambda b, h, i: (b, h, 0)),
            pl.BlockSpec((4, QK_DIM), lambda b, h, i: (0, 0)),
            pl.BlockSpec((V_DIM, TQ), lambda b, h, i: (0, 0)),
        ],
        out_specs=pl.BlockSpec((TQ, V_DIM), lambda b, h, i: (b * n_q + i, h)),
        out_shape=jax.ShapeDtypeStruct((batch * seq, D_MODEL), BF16),
        scratch_shapes=[pltpu.VMEM((V_DIM, 2 * TQ), F32)],
        compiler_params=_cparams(("arbitrary", "arbitrary", "arbitrary")),
        name="flash",
    )(qt_b, k_b, vt_b, lam_p, sub_g_b)


def _decode_kernel(pt_ref, q_ref, ktn_ref, vn_ref, lp_ref, sg_ref, ex_ref, hm_ref, *rest, lambda_init, n_new):
    del pt_ref
    npg = PAGES_PER_STEP
    k_refs = rest[:npg]
    v_refs = rest[npg:2 * npg]
    o_ref = rest[2 * npg]
    q2_s, m_s, l_s, acc_s = rest[2 * npg + 1:]
    p = pl.program_id(1)

    @pl.when(p == 0)
    def _():
        qq = jnp.concatenate([q_ref[...]] * (2 * N_HEADS), axis=0)
        row = lax.broadcasted_iota(jnp.int32, qq.shape, 0)
        lane = lax.broadcasted_iota(jnp.int32, qq.shape, 1)
        q2_s[...] = jnp.where(lane // QK_DIM == row // n_new, qq, 0.0).astype(BF16)
        m_s[...] = jnp.full_like(m_s, -jnp.inf)
        l_s[...] = jnp.zeros_like(l_s)
        acc_s[...] = jnp.zeros_like(acc_s)

    def update(s, pages):
        m_prev = m_s[...]
        m_new = jnp.maximum(m_prev, jnp.max(s, axis=-1, keepdims=True))
        alpha = jnp.exp2(m_prev - m_new)
        pr = jnp.exp2(s - m_new)
        l_s[...] = alpha * l_s[...] + jnp.sum(pr, axis=-1, keepdims=True)
        pr = pr.astype(BF16)
        pv = None
        for i, v in enumerate(pages):
            spread = jnp.dot(pr[:, i * PAGE_SIZE:(i + 1) * PAGE_SIZE], ex_ref[...], preferred_element_type=F32)
            part = jnp.dot((spread * hm_ref[...]).astype(BF16), v, preferred_element_type=F32)
            pv = part if pv is None else pv + part
        acc_s[...] = alpha * acc_s[...] + pv
        m_s[...] = m_new

    q2 = q2_s[...]
    s = jnp.concatenate(
        [jnp.dot(q2, k_refs[i][...].astype(BF16), preferred_element_type=F32) for i in range(npg)], axis=1)
    update(s, [v_refs[i][...].astype(BF16) for i in range(npg)])

    @pl.when(p == pl.num_programs(1) - 1)
    def _():
        sn = jnp.dot(q2, ktn_ref[...].astype(BF16), preferred_element_type=F32)
        row = lax.broadcasted_iota(jnp.int32, sn.shape, 0)
        col = lax.broadcasted_iota(jnp.int32, sn.shape, 1)
        sn = jnp.where(col <= row % n_new, sn, -jnp.inf)
        vn = vn_ref[...]
        vn = jnp.concatenate([vn, jnp.zeros((PAGE_SIZE * N_HEADS - vn.shape[0], V_DIM), F32)], axis=0)
        update(sn, [vn.astype(BF16)])
        lam = _lambda(lp_ref, lambda_init)
        on = acc_s[...] / l_s[...]
        for h in range(N_HEADS):
            blk = on[h * 2 * n_new:(h + 1) * 2 * n_new]
            o_ref[:, h * V_DIM:(h + 1) * V_DIM] = _head_out(blk[:n_new], blk[n_new:], lam, sg_ref[...],
                                                             lambda_init)


def _decode(pt_flat, q, ktn, v_s, lam_p, sub_g, ck_t, cv_r, layer, n_req, n_new, n_pages, lambda_init):
    npg = PAGES_PER_STEP
    n_steps = n_pages // npg
    n_cols = 2 * N_HEADS * n_new
    wide = PAGE_SIZE * N_HEADS
    spread = jnp.asarray(np.arange(PAGE_SIZE)[:, None] == np.arange(wide)[None, :] // N_HEADS, BF16)
    head_mask = jnp.asarray(np.arange(n_cols)[:, None] // (2 * n_new) == np.arange(wide)[None, :] % N_HEADS, F32)
    vn = v_s.reshape(n_req * n_new * N_HEADS, V_DIM)
    const = lambda b, p, pt: (0, 0)

    def page_spec(i):
        return pl.BlockSpec((None, None, D_MODEL, PAGE_SIZE),
                            lambda b, p, pt: (layer, pt[b * n_pages + p * npg + i], 0, 0))

    return pl.pallas_call(
        functools.partial(_decode_kernel, lambda_init=lambda_init, n_new=n_new),
        grid_spec=pltpu.PrefetchScalarGridSpec(
            num_scalar_prefetch=1,
            grid=(n_req, n_steps),
            in_specs=[
                pl.BlockSpec((n_new, D_MODEL), lambda b, p, pt: (b, 0)),
                pl.BlockSpec((None, D_MODEL, PAGE_SIZE), lambda b, p, pt: (b, 0, 0)),
                pl.BlockSpec((n_new * N_HEADS, V_DIM), lambda b, p, pt: (b, 0)),
                pl.BlockSpec((4, QK_DIM), const),
                pl.BlockSpec((1, V_DIM), const),
                pl.BlockSpec((PAGE_SIZE, wide), const),
                pl.BlockSpec((n_cols, wide), const),
            ] + [page_spec(i) for i in range(npg)] + [page_spec(i) for i in range(npg)],
            out_specs=pl.BlockSpec((n_new, D_MODEL), lambda b, p, pt: (b, 0)),
            scratch_shapes=[pltpu.VMEM((n_cols, D_MODEL), BF16), pltpu.VMEM((n_cols, 1), F32),
                            pltpu.VMEM((n_cols, 1), F32), pltpu.VMEM((n_cols, V_DIM), F32)],
        ),
        out_shape=jax.ShapeDtypeStruct((n_req * n_new, D_MODEL), F32),
        compiler_params=_cparams(("arbitrary", "arbitrary")),
        name="decode_attn",
    )(pt_flat, q, ktn, vn, lam_p, sub_g, spread, head_mask, *([ck_t] * npg), *([cv_r] * npg))


def _lru_kernel(u_ref, buf_ref, h0_ref, cw_ref, cb_ref, wa_ref, ba_ref, wi_ref, bi_ref, lam_ref,
                y_ref, hl_ref, ct_ref, xbuf, a_s, b_s, hc, *, ts):
    t = pl.program_id(1)

    @pl.when(t == 0)
    def _():
        xbuf[0:8, :] = buf_ref[...]
        hc[...] = jnp.broadcast_to(h0_ref[...], (8, D_RNN))

    xr = u_ref[:, D_RNN:]
    xbuf[8:8 + ts, :] = xr
    cw = cw_ref[...]
    xc = cb_ref[...] + (cw[0:1] * xbuf[5:5 + ts, :] + cw[1:2] * xbuf[6:6 + ts, :]
                        + cw[2:3] * xbuf[7:7 + ts, :] + cw[3:4] * xr)
    z = -lam_ref[...]
    sp = jnp.maximum(z, 0.0) + jnp.log1p(jnp.exp(-jnp.abs(z)))
    row = lax.broadcasted_iota(jnp.int32, (ts, RG_BLOCK), 0) % 8
    for n in range(N_RG_BLOCKS):
        sl = slice(n * RG_BLOCK, (n + 1) * RG_BLOCK)
        xcn = xc[:, sl]
        xb = xcn.astype(BF16)
        r = jax.nn.sigmoid(jnp.dot(xb, wa_ref[n], preferred_element_type=F32) + ba_ref[:, sl])
        ig = jax.nn.sigmoid(jnp.dot(xb, wi_ref[n], preferred_element_type=F32) + bi_ref[:, sl])
        log_a = -RG_C * r * sp[:, sl]
        a = jnp.exp(log_a)
        mult = jnp.sqrt(-jnp.tanh(log_a) * (a * a + 1.0))
        b = mult * (ig * xcn)
        for s in (1, 2, 4):
            a_sh = pltpu.roll(a, s, 0)
            b_sh = pltpu.roll(b, s, 0)
            keep = row >= s
            b = jnp.where(keep, a * b_sh + b, b)
            a = jnp.where(keep, a * a_sh, a)
        a_s[:, sl] = a
        b_s[:, sl] = b

    def step(i, h):
        r0 = pl.multiple_of(i * 8, 8)
        ht = a_s[pl.ds(r0, 8), :] * h + b_s[pl.ds(r0, 8), :]
        b_s[pl.ds(r0, 8), :] = ht
        return jnp.broadcast_to(ht[7:8, :], (8, D_RNN))

    h = lax.fori_loop(0, ts // 8, step, hc[...])
    hc[...] = h
    xbuf[0:8, :] = xbuf[ts:ts + 8, :]
    y_ref[...] = (b_s[...] * jax.nn.gelu(u_ref[:, :D_RNN])).astype(y_ref.dtype)

    @pl.when(t == pl.num_programs(1) - 1)
    def _():
        hl_ref[...] = h[0:1]
        ct_ref[...] = xbuf[0:8, :]


def _lru(u, buf0, h0, cw, cb, wa, ba, wi, bi, lam, n_seq, seq, ts, row_blk0, y_dtype):
    nt = seq // ts
    c2 = lambda b, t: (0, 0)
    c3 = lambda b, t: (0, 0, 0)
    return pl.pallas_call(
        functools.partial(_lru_kernel, ts=ts),
        grid=(n_seq, nt),
        in_specs=[
            pl.BlockSpec((ts, 2 * D_RNN), lambda b, t: (row_blk0 + b * nt + t, 0)),
            pl.BlockSpec((None, 8, D_RNN), lambda b, t: (b, 0, 0)),
            pl.BlockSpec((None, 1, D_RNN), lambda b, t: (b, 0, 0)),
            pl.BlockSpec((CONV_W, D_RNN), c2),
            pl.BlockSpec((1, D_RNN), c2),
            pl.BlockSpec((N_RG_BLOCKS, RG_BLOCK, RG_BLOCK), c3),
            pl.BlockSpec((1, D_RNN), c2),
            pl.BlockSpec((N_RG_BLOCKS, RG_BLOCK, RG_BLOCK), c3),
            pl.BlockSpec((1, D_RNN), c2),
            pl.BlockSpec((1, D_RNN), c2),
        ],
        out_specs=[
            pl.BlockSpec((ts, D_RNN), lambda b, t: (b * nt + t, 0)),
            pl.BlockSpec((None, 1, D_RNN), lambda b, t: (b, 0, 0)),
            pl.BlockSpec((None, 8, D_RNN), lambda b, t: (b, 0, 0)),
        ],
        out_shape=[
            jax.ShapeDtypeStruct((n_seq * seq, D_RNN), y_dtype),
            jax.ShapeDtypeStruct((n_seq, 1, D_RNN), F32),
            jax.ShapeDtypeStruct((n_seq, 8, D_RNN), F32),
        ],
        scratch_shapes=[pltpu.VMEM((ts + 8, D_RNN), F32), pltpu.VMEM((ts, D_RNN), F32),
                        pltpu.VMEM((ts, D_RNN), F32), pltpu.VMEM((8, D_RNN), F32)],
        compiler_params=_cparams(("arbitrary", "arbitrary")),
        name="lru",
    )(u, buf0, h0, cw, cb, wa, ba, wi, bi, lam)


def _dispatch(info, counts_row):
    t_all = info.shape[0]
    eid = info[:, _R_E1:_R_E2 + 1].astype(jnp.int32)
    rank = info[:, _R_RANK1:_R_RANK2 + 1].astype(jnp.int32)
    counts = counts_row[0, _EXPERT_LANE0:_EXPERT_LANE0 + N_EXPERTS].astype(jnp.int32)
    pcounts = ((counts + MOE_TM - 1) // MOE_TM) * MOE_TM
    pends = jnp.cumsum(pcounts)
    pstarts = pends - pcounts
    dest = pstarts.at[eid].get(mode='promise_in_bounds') + rank
    n_blk = (t_all * TOP_K) // MOE_TM + N_EXPERTS
    blk_row0 = jnp.arange(n_blk, dtype=jnp.int32) * MOE_TM
    blk_e = jnp.minimum(jnp.sum((pends[None, :] <= blk_row0[:, None]).astype(jnp.int32), axis=1), N_EXPERTS - 1)
    n_valid = (pends[-1] // MOE_TM).astype(jnp.int32).reshape(1)
    tok = jnp.broadcast_to(jnp.arange(t_all, dtype=jnp.int32)[:, None], (t_all, TOP_K))
    row_tok = (jnp.arange(n_blk * MOE_TM, dtype=jnp.int32) % t_all).at[dest.reshape(-1)].set(
        tok.reshape(-1), mode='promise_in_bounds', unique_indices=True)
    return dest, blk_e.astype(jnp.int32), n_valid, row_tok


def kernel(x_prompt, x_sample, cache_k, cache_v, state_h, state_conv, page_table, p_prompt, p_sample, mix_norm, ffn_norm, ple_norm, w_qkv, q_norm, k_norm, lambda_q1, lambda_k1, lambda_q2, lambda_k2, sub_norm, w_o, w_in_lru, conv_w, conv_b, w_rg_a, b_rg_a, w_rg_i, b_rg_i, lru_lambda, w_out_lru, w_group, b_group, w_sub, b_sub, w_gate_up, w_down, w_ple_gate, w_ple_proj):
    batch, seq, _ = x_prompt.shape
    n_req, n_new, _ = x_sample.shape
    n_pages = page_table.shape[1]
    past_len = n_pages * PAGE_SIZE
    t_p = batch * seq
    t_s = n_req * n_new
    assert t_s == TM and t_p % TM == 0 and seq % TM == 0 and n_pages % PAGES_PER_STEP == 0
    assert (t_p + t_s) % TM_WIDE == 0 and seq % TQ == 0
    n_p = t_p // TM
    n_attn = cache_k.shape[0]
    n_pool = cache_k.shape[1]

    h = jnp.concatenate([x_prompt.reshape(t_p, D_MODEL), x_sample.reshape(t_s, D_MODEL)], axis=0)

    pos = jnp.concatenate([jnp.tile(jnp.arange(seq, dtype=jnp.int32), batch),
                           jnp.tile(past_len + jnp.arange(n_new, dtype=jnp.int32), n_req)])
    half = ROPE_DIM // 2
    inv = ROPE_THETA ** (-(jnp.arange(half, dtype=F32) * 2.0 / ROPE_DIM))
    ang = pos.astype(F32)[:, None] * inv[None, :]
    cos_t = jnp.cos(ang).T
    sin_t = jnp.sin(ang).T

    ck_t = cache_k.transpose(0, 1, 3, 4, 5, 2).reshape(n_attn, n_pool, D_MODEL, PAGE_SIZE)
    cv_r = cache_v.reshape(n_attn, n_pool, PAGE_SIZE * N_HEADS, V_DIM)
    pt_flat = page_table.reshape(-1).astype(jnp.int32)
    pp = p_prompt.reshape(DEPTH, t_p, PLE_DIM)
    ps = p_sample.reshape(DEPTH, t_s, PLE_DIM)

    k_s, v_s, hl_p, ct_p, hl_s, ct_s = [], [], [], [], [], []
    y_p = y_s = kv_prev = None
    for i in range(DEPTH):
        j = i // 2
        g_mix = mix_norm[i].reshape(1, D_MODEL).astype(F32)
        if i % 2 == 0:
            lambda_init = 0.8 - 0.6 * math.exp(-0.3 * i)
            wq_t = w_qkv[j, :, :D_MODEL].T.astype(BF16)
            wk_t = w_qkv[j, :, D_MODEL:2 * D_MODEL].T.astype(BF16)
            wv = w_qkv[j, :, 2 * D_MODEL:].astype(BF16)
            qg_b = jnp.broadcast_to(q_norm[j].astype(F32)[:, None], (QK_DIM, TM))
            kg_b = jnp.broadcast_to(k_norm[j].astype(F32)[:, None], (QK_DIM, TM))
            qt_b, q_s, kt_p, kt_s, k_b, vv_p, vv_s, vt_b = _qkv(h, g_mix, wq_t, wk_t, wv, qg_b, kg_b, cos_t, sin_t,
                                                                n_p, batch, seq, j, n_attn, kv_prev)
            kv_prev = (kt_p, vv_p)
            lam_p = jnp.stack([lambda_q1[j], lambda_k1[j], lambda_q2[j], lambda_k2[j]]).astype(F32)
            sub_g = sub_norm[j].reshape(1, V_DIM).astype(F32)
            sub_g_b = jnp.broadcast_to(sub_norm[j].astype(F32)[:, None], (V_DIM, TQ))
            o_p = _flash(qt_b, k_b, vt_b, lam_p, sub_g_b, batch, seq, lambda_init)
            ktn = kt_s.reshape(D_MODEL, n_req, n_new).transpose(1, 0, 2)
            ktn = jnp.pad(ktn, ((0, 0), (0, 0), (0, PAGE_SIZE - n_new)))
            o_s = _decode(pt_flat, q_s, ktn, vv_s, lam_p, sub_g, ck_t, cv_r, j, n_req, n_new, n_pages,
                          lambda_init)
            h = _resid_matmul(h, o_p, o_s, w_o[j].astype(BF16), n_p)
            k_s.append(kt_s.T.reshape(n_req, n_new, N_HEADS, 2, QK_DIM))
            v_s.append(vv_s.reshape(n_req, n_new, N_HEADS, V_DIM))
        else:
            u = _norm_matmul(h, g_mix, w_in_lru[j].astype(BF16))
            cw = conv_w[j].astype(F32)
            cb = conv_b[j].reshape(1, D_RNN).astype(F32)
            wa = w_rg_a[j].astype(BF16)
            wi = w_rg_i[j].astype(BF16)
            ba = b_rg_a[j].reshape(1, D_RNN).astype(F32)
            bi = b_rg_i[j].reshape(1, D_RNN).astype(F32)
            lam = lru_lambda[j].reshape(1, D_RNN).astype(F32)
            zb = jnp.zeros((batch, 8, D_RNN), F32)
            zh = jnp.zeros((batch, 1, D_RNN), F32)
            yl_p, hlp, ctp = _lru(u, zb, zh, cw, cb, wa, ba, wi, bi, lam, batch, seq, LRU_TS, 0, BF16)
            buf_s = jnp.pad(state_conv[j].astype(F32), ((0, 0), (8 - (CONV_W - 1), 0), (0, 0)))
            h0_s = state_h[j].astype(F32)[:, None, :]
            yl_s, hls, cts = _lru(u, buf_s, h0_s, cw, cb, wa, ba, wi, bi, lam, n_req, n_new, n_new,
                                  t_p // n_new, F32)
            h = _resid_matmul(h, yl_p, yl_s, w_out_lru[j].astype(BF16), n_p)
            hl_p.append(hlp[:, 0])
            ct_p.append(ctp[:, 8 - (CONV_W - 1):])
            hl_s.append(hls[:, 0])
            ct_s.append(cts[:, 8 - (CONV_W - 1):])

        w_r = jnp.concatenate([w_group[i], w_sub[i]], axis=1).astype(F32)
        w_r = jnp.pad(w_r, ((0, 0), (0, 128 - w_r.shape[1])))
        w_hi = w_r.astype(BF16)
        w_lo = (w_r - w_hi.astype(F32)).astype(BF16)
        bias = jnp.concatenate([b_group[i], b_sub[i]]).astype(F32)
        bias = jnp.pad(bias, (0, _R_LANES - bias.shape[0])).reshape(1, _R_LANES)
        hn, info, counts = _router(h, ffn_norm[i].reshape(1, D_MODEL).astype(F32), w_hi, w_lo, bias)
        dest, blk_e, n_valid, row_tok = _dispatch(info, counts)
        xs = hn.at[row_tok].get(mode='promise_in_bounds')
        ys = _moe_ffn(blk_e, n_valid, xs, w_gate_up, w_down, i)
        ma = ys.at[dest[:, 0]].get(mode='promise_in_bounds')
        mb = ys.at[dest[:, 1]].get(mode='promise_in_bounds')

        last = i == DEPTH - 1
        out = _ple(h, ma, mb, info, ple_norm[i].reshape(1, D_MODEL).astype(F32), w_ple_gate[i].astype(BF16),
                   pp, ps, w_ple_proj[i].astype(BF16), i, n_p, last)
        if last:
            y_p, y_s = out
        else:
            h = out

    kt_all, v_all = kv_prev
    k_prompt = kt_all.reshape(n_attn, batch, N_HEADS, 2, QK_DIM, seq).transpose(0, 1, 5, 2, 3, 4)
    v_prompt = v_all.reshape(n_attn, batch, seq, N_HEADS, V_DIM)
    return (y_p.reshape(batch, seq, D_MODEL), y_s.reshape(n_req, n_new, D_MODEL),
            k_prompt, v_prompt, jnp.stack(hl_p), jnp.stack(ct_p),
            jnp.stack(k_s), jnp.stack(v_s), jnp.stack(hl_s), jnp.stack(ct_s))
```

```python
import functools
import math

import jax
import jax.numpy as jnp
import numpy as np
from jax import lax
from jax.experimental import pallas as pl
from jax.experimental.pallas import tpu as pltpu

F32 = jnp.float32
BF16 = jnp.bfloat16

D_MODEL = 1024
DEPTH = 4
PAGE_SIZE = 128
N_HEADS = 8
QK_DIM = 64
V_DIM = 128
ROPE_DIM = 16
ROPE_THETA = 500000.0
D_RNN = 1280
RG_BLOCK = 128
N_RG_BLOCKS = 10
CONV_W = 4
RG_C = 8.0
N_GROUPS = 4
EXPERTS_PER_GROUP = 8
N_EXPERTS = 32
TOP_K = 2
D_EXPERT = 512
PLE_DIM = 256
EPS = 1e-6

TM = 256
TM_WIDE = 640
TQ = 512
TK = 256
PAGES_PER_STEP = 8
MOE_TM = 512
LRU_TS = 256
VMEM_LIMIT = 56 * 1024 * 1024

_NT = (((1,), (1,)), ((), ()))
_STRIP = 128
_Q_SCALE = QK_DIM ** -0.5 * math.log2(math.e)


def _cparams(sem):
    return pltpu.CompilerParams(dimension_semantics=sem, vmem_limit_bytes=VMEM_LIMIT)


def _rms(x, g):
    return x * lax.rsqrt(jnp.mean(x * x, axis=-1, keepdims=True) + EPS) * g


def _tile(i, n_p):
    return jnp.where(i == 0, n_p, i - 1)


def _ptile(i):
    return jnp.maximum(i - 1, 0)


def _qkv_kernel(h_ref, g_ref, wq_ref, wk_ref, wv_ref, qg_ref, kg_ref, cos_ref, sin_ref, *rest):
    qtb_ref, qs_ref, ktp_ref, kts_ref, kb_ref, vp_ref, vs_ref, vtb_ref, qt_s, kt_s = rest[-10:]
    i = pl.program_id(0)
    hn = _rms(h_ref[...], g_ref[...]).astype(BF16)
    cos = cos_ref[...]
    sin = sin_ref[...]

    def normrope(w_ref, gain_ref, dst, scale):
        xt = lax.dot_general(w_ref[...], hn, _NT, preferred_element_type=F32)
        gain = gain_ref[...]
        for g in range(2 * N_HEADS):
            xg = xt[g * QK_DIM:(g + 1) * QK_DIM, :]
            ss = jnp.sum(xg * xg, axis=0, keepdims=True)
            yg = xg * lax.rsqrt(ss * (1.0 / QK_DIM) + EPS) * gain
            y1 = yg[0:8]
            y2 = yg[8:16]
            dst[g * QK_DIM:g * QK_DIM + 8, :] = (y1 * cos - y2 * sin) * scale
            dst[g * QK_DIM + 8:g * QK_DIM + 16, :] = (y2 * cos + y1 * sin) * scale
            dst[g * QK_DIM + 16:(g + 1) * QK_DIM, :] = yg[16:] * scale

    normrope(wq_ref, qg_ref, qt_s, _Q_SCALE)
    normrope(wk_ref, kg_ref, kt_s, 1.0)
    v = jnp.dot(hn, wv_ref[...], preferred_element_type=F32)

    @pl.when(i == 0)
    def _():
        qs_ref[...] = qt_s[...].T
        kts_ref[...] = kt_s[...]
        vs_ref[...] = v

    @pl.when(i > 0)
    def _():
        kt = kt_s[...]
        qtb_ref[...] = qt_s[...].astype(BF16)
        ktp_ref[...] = kt
        kb_ref[...] = kt.T.astype(BF16)
        vp_ref[...] = v
        vtb_ref[...] = v.T.astype(BF16)


def _qkv(h, g, wq_t, wk_t, wv, qg_b, kg_b, cos_t, sin_t, n_p, batch, seq, layer, n_attn, prev):
    n_sb = seq // TM
    const = lambda i: (0, 0)
    tile = lambda i: (_tile(i, n_p), 0)
    ktile = lambda i: (_ptile(i) // n_sb, 0, _ptile(i) % n_sb)
    ltile = lambda i: (layer, _ptile(i) // n_sb, 0, _ptile(i) % n_sb)
    prev = () if prev is None else tuple(prev)
    n_in = 9
    return pl.pallas_call(
        _qkv_kernel,
        grid=(n_p + 1,),
        in_specs=[
            pl.BlockSpec((TM, D_MODEL), tile),
            pl.BlockSpec((1, D_MODEL), const),
            pl.BlockSpec((D_MODEL, D_MODEL), const),
            pl.BlockSpec((D_MODEL, D_MODEL), const),
            pl.BlockSpec((D_MODEL, D_MODEL), const),
            pl.BlockSpec((QK_DIM, TM), const),
            pl.BlockSpec((QK_DIM, TM), const),
            pl.BlockSpec((8, TM), lambda i: (0, _tile(i, n_p))),
            pl.BlockSpec((8, TM), lambda i: (0, _tile(i, n_p))),
        ] + [pl.BlockSpec(memory_space=pl.ANY)] * len(prev),
        out_specs=[
            pl.BlockSpec((None, D_MODEL, TM), ktile),
            pl.BlockSpec((TM, D_MODEL), const),
            pl.BlockSpec((None, None, D_MODEL, TM), ltile),
            pl.BlockSpec((D_MODEL, TM), const),
            pl.BlockSpec((TM, D_MODEL), lambda i: (_ptile(i), 0)),
            pl.BlockSpec((None, TM, D_MODEL), lambda i: (layer, _ptile(i), 0)),
            pl.BlockSpec((TM, D_MODEL), const),
            pl.BlockSpec((None, D_MODEL, TM), ktile),
        ],
        out_shape=[
            jax.ShapeDtypeStruct((batch, D_MODEL, seq), BF16),
            jax.ShapeDtypeStruct((TM, D_MODEL), F32),
            jax.ShapeDtypeStruct((n_attn, batch, D_MODEL, seq), F32),
            jax.ShapeDtypeStruct((D_MODEL, TM), F32),
            jax.ShapeDtypeStruct((n_p * TM, D_MODEL), BF16),
            jax.ShapeDtypeStruct((n_attn, n_p * TM, D_MODEL), F32),
            jax.ShapeDtypeStruct((TM, D_MODEL), F32),
            jax.ShapeDtypeStruct((batch, D_MODEL, seq), BF16),
        ],
        input_output_aliases={n_in: 2, n_in + 1: 5} if prev else {},
        scratch_shapes=[pltpu.VMEM((D_MODEL, TM), F32), pltpu.VMEM((D_MODEL, TM), F32)],
        compiler_params=_cparams(("arbitrary",)),
        name="qkv",
    )(h, g, wq_t, wk_t, wv, qg_b, kg_b, cos_t, sin_t, *prev)


def _resid_matmul_kernel(h_ref, xp_ref, xs_ref, w_ref, o_ref):
    i = pl.program_id(0)
    x = jnp.where(i == 0, xs_ref[...].astype(BF16), xp_ref[...].astype(BF16))
    o_ref[...] = h_ref[...] + jnp.dot(x, w_ref[...], preferred_element_type=F32)


def _resid_matmul(h, x_p, x_s, w, n_p):
    t_all = h.shape[0]
    k = w.shape[0]
    tile = lambda i: (_tile(i, n_p), 0)
    return pl.pallas_call(
        _resid_matmul_kernel,
        grid=(n_p + 1,),
        in_specs=[
            pl.BlockSpec((TM, D_MODEL), tile),
            pl.BlockSpec((TM, k), lambda i: (_ptile(i), 0)),
            pl.BlockSpec((TM, k), lambda i: (0, 0)),
            pl.BlockSpec((k, D_MODEL), lambda i: (0, 0)),
        ],
        out_specs=pl.BlockSpec((TM, D_MODEL), tile),
        out_shape=jax.ShapeDtypeStruct((t_all, D_MODEL), F32),
        compiler_params=_cparams(("arbitrary",)),
        name="resid_matmul",
    )(h, x_p, x_s, w)


def _norm_matmul_kernel(h_ref, g_ref, w_ref, o_ref):
    hn = _rms(h_ref[...], g_ref[...]).astype(BF16)
    o_ref[...] = jnp.dot(hn, w_ref[...], preferred_element_type=F32)


def _norm_matmul(h, g, w):
    t_all = h.shape[0]
    n = w.shape[1]
    return pl.pallas_call(
        _norm_matmul_kernel,
        grid=(t_all // TM_WIDE,),
        in_specs=[
            pl.BlockSpec((TM_WIDE, D_MODEL), lambda i: (i, 0)),
            pl.BlockSpec((1, D_MODEL), lambda i: (0, 0)),
            pl.BlockSpec((D_MODEL, n), lambda i: (0, 0)),
        ],
        out_specs=pl.BlockSpec((TM_WIDE, n), lambda i: (i, 0)),
        out_shape=jax.ShapeDtypeStruct((t_all, n), F32),
        compiler_params=_cparams(("arbitrary",)),
        name="norm_matmul",
    )(h, g, w)


_R_E1, _R_E2, _R_W1, _R_W2, _R_RANK1, _R_RANK2 = range(6)
_R_LANES = 128
_EXPERT_LANE0 = N_GROUPS


def _router_kernel(h_ref, g_ref, whi_ref, wlo_ref, bias_ref, hn_ref, info_ref, cnt_ref, cnt_s):
    i = pl.program_id(0)

    @pl.when(i == 0)
    def _():
        cnt_s[...] = jnp.zeros_like(cnt_s)

    hn = _rms(h_ref[...], g_ref[...])
    hi = hn.astype(BF16)
    bits = pltpu.bitcast(hi.astype(F32), jnp.uint32)
    hn_ref[...] = (bits[:, :D_MODEL // 2] >> 16) | bits[:, D_MODEL // 2:]
    lo = (hn - hi.astype(F32)).astype(BF16)
    whi = whi_ref[...]
    lg = jnp.dot(hi, whi, preferred_element_type=F32)
    lg += jnp.dot(lo, whi, preferred_element_type=F32)
    lg += jnp.dot(hi, wlo_ref[...], preferred_element_type=F32)
    lg += bias_ref[...]

    lane = lax.broadcasted_iota(jnp.int32, lg.shape, 1).astype(F32)
    far = float(_R_LANES)
    rmax = lambda x: jnp.max(x, axis=-1, keepdims=True)
    rmin = lambda x: jnp.min(x, axis=-1, keepdims=True)
    rsum = lambda x: jnp.sum(x, axis=-1, keepdims=True)

    gl = jnp.where(lane < N_GROUPS, lg, -jnp.inf)
    gmax = rmax(gl)
    ge = jnp.exp(gl - gmax)
    gp = ge / rsum(ge)
    g_idx = rmin(jnp.where(gl == gmax, lane, far))
    g_w = rsum(jnp.where(lane == g_idx, gp, 0.0))

    lane0 = _EXPERT_LANE0 + g_idx * EXPERTS_PER_GROUP
    smask = (lane >= lane0) & (lane < lane0 + EXPERTS_PER_GROUP)
    sl = jnp.where(smask, lg, -jnp.inf)
    se = jnp.exp(sl - rmax(sl))
    sp = jnp.where(smask, se / rsum(se), -1.0)
    v1 = rmax(sp)
    i1 = rmin(jnp.where(sp == v1, lane, far))
    sp2 = jnp.where(lane == i1, -1.0, sp)
    v2 = rmax(sp2)
    i2 = rmin(jnp.where(sp2 == v2, lane, far))
    den = v1 + v2
    w1 = g_w * (v1 / den)
    w2 = g_w * (v2 / den)

    oh1 = jnp.where(lane == i1, 1.0, 0.0)
    oh2 = jnp.where(lane == i2, 1.0, 0.0)
    r = lax.broadcasted_iota(jnp.int32, (TM_WIDE, TM_WIDE), 0)
    c = lax.broadcasted_iota(jnp.int32, (TM_WIDE, TM_WIDE), 1)
    tri = jnp.where(c < r, 1.0, 0.0).astype(BF16)
    pre1 = jnp.dot(tri, oh1.astype(BF16), preferred_element_type=F32)
    pre2 = jnp.dot(tri, oh2.astype(BF16), preferred_element_type=F32)
    c1 = jnp.sum(oh1, axis=0, keepdims=True)
    c2 = jnp.sum(oh2, axis=0, keepdims=True)
    base = cnt_s[...]
    rank1 = rsum(oh1 * (pre1 + base))
    rank2 = rsum(oh2 * (pre2 + (base + c1)))
    total = base + (c1 + c2)
    cnt_s[...] = total
    cnt_ref[...] = total

    info = jnp.zeros_like(lg)
    for k, val in ((_R_E1, i1 - _EXPERT_LANE0), (_R_E2, i2 - _EXPERT_LANE0), (_R_W1, w1), (_R_W2, w2),
                   (_R_RANK1, rank1), (_R_RANK2, rank2)):
        info = jnp.where(lane == float(k), val, info)
    info_ref[...] = info


def _router(h, g, w_hi, w_lo, bias):
    t_all = h.shape[0]
    const = lambda i: (0, 0)
    return pl.pallas_call(
        _router_kernel,
        grid=(t_all // TM_WIDE,),
        in_specs=[
            pl.BlockSpec((TM_WIDE, D_MODEL), lambda i: (i, 0)),
            pl.BlockSpec((1, D_MODEL), const),
            pl.BlockSpec((D_MODEL, _R_LANES), const),
            pl.BlockSpec((D_MODEL, _R_LANES), const),
            pl.BlockSpec((1, _R_LANES), const),
        ],
        out_specs=[
            pl.BlockSpec((TM_WIDE, D_MODEL // 2), lambda i: (i, 0)),
            pl.BlockSpec((TM_WIDE, _R_LANES), lambda i: (i, 0)),
            pl.BlockSpec((1, _R_LANES), const),
        ],
        out_shape=[
            jax.ShapeDtypeStruct((t_all, D_MODEL // 2), jnp.uint32),
            jax.ShapeDtypeStruct((t_all, _R_LANES), F32),
            jax.ShapeDtypeStruct((1, _R_LANES), F32),
        ],
        scratch_shapes=[pltpu.VMEM((1, _R_LANES), F32)],
        compiler_params=_cparams(("arbitrary",)),
        name="router",
    )(h, g, w_hi, w_lo, bias)


def _moe_ffn_kernel(be_ref, nv_ref, x_ref, wgu_ref, wd_ref, o_ref, wgu_s, wd_s):
    j = pl.program_id(0)
    valid = j < nv_ref[0]
    new_expert = (j == 0) | (be_ref[j] != be_ref[jnp.maximum(j - 1, 0)])

    @pl.when(valid & new_expert)
    def _():
        wgu_s[...] = wgu_ref[...].astype(BF16)
        wd_s[...] = wd_ref[...].astype(BF16)

    @pl.when(valid)
    def _():
        words = x_ref[...]
        x = jnp.concatenate([pltpu.bitcast(words << 16, F32),
                             pltpu.bitcast(words & jnp.uint32(0xFFFF0000), F32)], axis=1).astype(BF16)
        gu = jnp.dot(x, wgu_s[...], preferred_element_type=F32)
        act = jax.nn.silu(gu[:, :D_EXPERT]) * gu[:, D_EXPERT:]
        o_ref[...] = jnp.dot(act.astype(BF16), wd_s[...], preferred_element_type=F32)

    @pl.when(jnp.logical_not(valid))
    def _():
        o_ref[...] = jnp.zeros_like(o_ref)


def _moe_ffn(blk_e, n_valid, xs, wgu, wd, layer):
    n_rows = xs.shape[0]
    n_blk = n_rows // MOE_TM
    return pl.pallas_call(
        _moe_ffn_kernel,
        grid_spec=pltpu.PrefetchScalarGridSpec(
            num_scalar_prefetch=2,
            grid=(n_blk,),
            in_specs=[
                pl.BlockSpec((MOE_TM, D_MODEL // 2), lambda j, be, nv: (j, 0)),
                pl.BlockSpec((None, None, D_MODEL, 2 * D_EXPERT), lambda j, be, nv: (layer, be[j], 0, 0)),
                pl.BlockSpec((None, None, D_EXPERT, D_MODEL), lambda j, be, nv: (layer, be[j], 0, 0)),
            ],
            out_specs=pl.BlockSpec((MOE_TM, D_MODEL), lambda j, be, nv: (j, 0)),
            scratch_shapes=[pltpu.VMEM((D_MODEL, 2 * D_EXPERT), BF16), pltpu.VMEM((D_EXPERT, D_MODEL), BF16)],
        ),
        out_shape=jax.ShapeDtypeStruct((n_rows, D_MODEL), F32),
        compiler_params=_cparams(("arbitrary",)),
        name="moe_ffn",
    )(blk_e, n_valid, xs, wgu, wd)


def _ple_kernel(h_ref, ma_ref, mb_ref, info_ref, g_ref, wg_ref, pp_ref, ps_ref, wp_ref, *o_refs, split):
    i = pl.program_id(0)
    info = info_ref[...]
    w1 = info[:, _R_W1:_R_W1 + 1]
    w2 = info[:, _R_W2:_R_W2 + 1]
    x = h_ref[...] + (w1 * ma_ref[...] + w2 * mb_ref[...])
    hn = _rms(x, g_ref[...]).astype(BF16)
    gate = jax.nn.sigmoid(jnp.dot(hn, wg_ref[...], preferred_element_type=F32))
    p = jnp.where(i == 0, ps_ref[...], pp_ref[...]).astype(BF16)
    out = x + gate * jnp.dot(p, wp_ref[...], preferred_element_type=F32)
    if split:
        yp_ref, ys_ref = o_refs

        @pl.when(i == 0)
        def _():
            ys_ref[...] = out

        @pl.when(i > 0)
        def _():
            yp_ref[...] = out
    else:
        o_refs[0][...] = out


def _ple(h, ma, mb, info, g, wg, p_p, p_s, wp, layer, n_p, split):
    t_all = h.shape[0]
    tile = lambda i: (_tile(i, n_p), 0)
    const = lambda i: (0, 0)
    if split:
        out_specs = [pl.BlockSpec((TM, D_MODEL), lambda i: (_ptile(i), 0)),
                     pl.BlockSpec((TM, D_MODEL), const)]
        out_shape = [jax.ShapeDtypeStruct((n_p * TM, D_MODEL), F32),
                     jax.ShapeDtypeStruct((TM, D_MODEL), F32)]
    else:
        out_specs = pl.BlockSpec((TM, D_MODEL), tile)
        out_shape = jax.ShapeDtypeStruct((t_all, D_MODEL), F32)
    return pl.pallas_call(
        functools.partial(_ple_kernel, split=split),
        grid=(n_p + 1,),
        in_specs=[
            pl.BlockSpec((TM, D_MODEL), tile),
            pl.BlockSpec((TM, D_MODEL), tile),
            pl.BlockSpec((TM, D_MODEL), tile),
            pl.BlockSpec((TM, _R_LANES), tile),
            pl.BlockSpec((1, D_MODEL), const),
            pl.BlockSpec((D_MODEL, D_MODEL), const),
            pl.BlockSpec((None, TM, PLE_DIM), lambda i: (layer, _ptile(i), 0)),
            pl.BlockSpec((None, TM, PLE_DIM), lambda i: (layer, 0, 0)),
            pl.BlockSpec((PLE_DIM, D_MODEL), const),
        ],
        out_specs=out_specs,
        out_shape=out_shape,
        compiler_params=_cparams(("arbitrary",)),
        name="ple",
    )(h, ma, mb, info, g, wg, p_p, p_s, wp)


def _lambda(lp_ref, lambda_init):
    lp = lp_ref[...]
    s1 = jnp.sum(lp[0:1] * lp[1:2], axis=-1, keepdims=True)
    s2 = jnp.sum(lp[2:3] * lp[3:4], axis=-1, keepdims=True)
    return jnp.exp(s1) - jnp.exp(s2) + lambda_init


def _head_out(o0, o1, lam, sg, lambda_init):
    o = o0 - lam * o1
    return _rms(o, sg) * (1.0 - lambda_init)


def _flash_body(qi, qt_ref, k_ref, vt_ref, lp_ref, sg_ref, o_ref, acc_s, lambda_init):
    qt = qt_ref[...]
    zero = jnp.zeros((QK_DIM, TQ), qt.dtype)
    q2t = jnp.concatenate([jnp.concatenate([qt[:QK_DIM], zero], axis=0),
                           jnp.concatenate([zero, qt[QK_DIM:]], axis=0)], axis=1)
    acc_s[...] = jnp.zeros_like(acc_s)
    n_strip = 2 * TQ // _STRIP
    tiles_per_q = TQ // TK

    def attend(j, carry, diag):
        c0 = pl.multiple_of(j * TK, TK)
        kj = k_ref[pl.ds(c0, TK), :]
        vtj = vt_ref[:, pl.ds(c0, TK)]
        out = []
        for c in range(n_strip):
            cs = slice(c * _STRIP, (c + 1) * _STRIP)
            tok0 = (c * _STRIP) % TQ
            key0 = 0 if diag is None else diag * TK
            if diag is not None and key0 > tok0 + _STRIP - 1:
                out.append(carry[c])
                continue
            m_prev, l_prev = carry[c]
            st = jnp.dot(kj, q2t[:, cs], preferred_element_type=F32)
            if diag is not None and key0 + TK - 1 > tok0:
                key = lax.broadcasted_iota(jnp.int32, st.shape, 0) + key0
                tok = lax.broadcasted_iota(jnp.int32, st.shape, 1) + tok0
                st = jnp.where(key <= tok, st, -jnp.inf)
            m_new = jnp.maximum(m_prev, jnp.max(st, axis=0, keepdims=True))
            alpha = jnp.exp2(m_prev - m_new)
            p = jnp.exp2(st - m_new)
            l_new = alpha * l_prev + jnp.sum(p, axis=0, keepdims=True)
            pv = jnp.dot(vtj, p.astype(BF16), preferred_element_type=F32)
            acc_s[:, cs] = alpha * acc_s[:, cs] + pv
            out.append((m_new, l_new))
        return tuple(out)

    def group(jj, carry):
        for u in range(tiles_per_q):
            carry = attend(tiles_per_q * jj + u, carry, None)
        return carry

    init = tuple((jnp.full((1, _STRIP), -jnp.inf, F32), jnp.zeros((1, _STRIP), F32)) for _ in range(n_strip))
    carry = lax.fori_loop(0, qi, group, init)
    for d in range(tiles_per_q):
        carry = attend(tiles_per_q * qi + d, carry, d)

    on = acc_s[...] * (1.0 / jnp.concatenate([l for _, l in carry], axis=1))
    lam = _lambda(lp_ref, lambda_init)
    ot = on[:, :TQ] - lam * on[:, TQ:]
    y = ot * lax.rsqrt(jnp.mean(ot * ot, axis=0, keepdims=True) + EPS) * sg_ref[...]
    o_ref[...] = (y * (1.0 - lambda_init)).T.astype(o_ref.dtype)


def _attn_kernel(pt_ref, qt_ref, k_ref, vt_ref, lp_ref, sgb_ref, q_ref, ktn_ref, vn_ref, sg_ref, *rest,
                 lambda_init, n_new):
    del pt_ref
    npg = PAGES_PER_STEP
    k_refs = rest[:npg]
    v_refs = rest[npg:2 * npg]
    op_ref, os_ref = rest[2 * npg:2 * npg + 2]
    acc_p, q2_s, m_s, l_s, acc_s = rest[2 * npg + 2:]
    step = pl.program_id(2)
    _decode_body(step, pl.num_programs(2), q_ref, ktn_ref, vn_ref, lp_ref, sg_ref, k_refs, v_refs, os_ref,
                 q2_s, m_s, l_s, acc_s, lambda_init, n_new)
    _flash_body(step, qt_ref, k_ref, vt_ref, lp_ref, sgb_ref, op_ref, acc_p, lambda_init)


def _attention(pt_flat, qt_b, k_b, vt_b, lam_p, sub_g_b, q_s, ktn, v_s, sub_g, ck_t, cv_r, layer,
               batch, seq, n_req, n_new, n_pages, lambda_init):
    npg = PAGES_PER_STEP
    n_q = seq // TQ
    assert n_req == batch * N_HEADS and n_pages == n_q * npg
    n_cols = 2 * N_HEADS * n_new
    const = lambda b, h, i, pt: (0, 0)
    req = lambda b, h: b * N_HEADS + h

    def page_spec(g):
        return pl.BlockSpec((None, None, D_MODEL, PAGE_SIZE),
                            lambda b, h, i, pt: (layer, pt[req(b, h) * n_pages + i * npg + g], 0, 0))

    return pl.pallas_call(
        functools.partial(_attn_kernel, lambda_init=lambda_init, n_new=n_new),
        grid_spec=pltpu.PrefetchScalarGridSpec(
            num_scalar_prefetch=1,
            grid=(batch, N_HEADS, n_q),
            in_specs=[
                pl.BlockSpec((None, V_DIM, TQ), lambda b, h, i, pt: (b, h, i)),
                pl.BlockSpec((seq, V_DIM), lambda b, h, i, pt: (b, h)),
                pl.BlockSpec((None, V_DIM, seq), lambda b, h, i, pt: (b, h, 0)),
                pl.BlockSpec((4, QK_DIM), const),
                pl.BlockSpec((V_DIM, TQ), const),
                pl.BlockSpec((n_new, D_MODEL), lambda b, h, i, pt: (req(b, h), 0)),
                pl.BlockSpec((None, D_MODEL, PAGE_SIZE), lambda b, h, i, pt: (req(b, h), 0, 0)),
                pl.BlockSpec((n_new, D_MODEL), lambda b, h, i, pt: (req(b, h), 0)),
                pl.BlockSpec((1, V_DIM), const),
            ] + [page_spec(g) for g in range(npg)] + [page_spec(g) for g in range(npg)],
            out_specs=[
                pl.BlockSpec((TQ, V_DIM), lambda b, h, i, pt: (b * n_q + i, h)),
                pl.BlockSpec((n_new, D_MODEL), lambda b, h, i, pt: (req(b, h), 0)),
            ],
            scratch_shapes=[pltpu.VMEM((V_DIM, 2 * TQ), F32),
                            pltpu.VMEM((n_cols, D_MODEL), BF16), pltpu.VMEM((n_cols, 1), F32),
                            pltpu.VMEM((n_cols, 1), F32), pltpu.VMEM((n_cols, D_MODEL), F32)],
        ),
        out_shape=[jax.ShapeDtypeStruct((batch * seq, D_MODEL), BF16),
                   jax.ShapeDtypeStruct((n_req * n_new, D_MODEL), F32)],
        compiler_params=_cparams(("arbitrary", "arbitrary", "arbitrary")),
        name="attention",
    )(pt_flat, qt_b, k_b, vt_b, lam_p, sub_g_b, q_s, ktn, v_s, sub_g, *([ck_t] * npg), *([cv_r] * npg))


def _decode_body(p, n_steps, q_ref, ktn_ref, vn_ref, lp_ref, sg_ref, k_refs, v_refs, o_ref,
                 q2_s, m_s, l_s, acc_s, lambda_init, n_new):
    npg = len(k_refs)

    @pl.when(p == 0)
    def _():
        qq = jnp.concatenate([q_ref[...]] * (2 * N_HEADS), axis=0)
        row = lax.broadcasted_iota(jnp.int32, qq.shape, 0)
        lane = lax.broadcasted_iota(jnp.int32, qq.shape, 1)
        q2_s[...] = jnp.where(lane // QK_DIM == row // n_new, qq, 0.0).astype(BF16)
        m_s[...] = jnp.full_like(m_s, -jnp.inf)
        l_s[...] = jnp.zeros_like(l_s)
        acc_s[...] = jnp.zeros_like(acc_s)

    def update(s, v):
        m_prev = m_s[...]
        m_new = jnp.maximum(m_prev, jnp.max(s, axis=-1, keepdims=True))
        alpha = jnp.exp2(m_prev - m_new)
        pr = jnp.exp2(s - m_new)
        l_s[...] = alpha * l_s[...] + jnp.sum(pr, axis=-1, keepdims=True)
        acc_s[...] = alpha * acc_s[...] + jnp.dot(pr.astype(BF16), v, preferred_element_type=F32)
        m_s[...] = m_new

    q2 = q2_s[...]
    s = jnp.concatenate(
        [jnp.dot(q2, k_refs[g][...].astype(BF16), preferred_element_type=F32) for g in range(npg)], axis=1)
    v = jnp.concatenate(
        [jnp.concatenate([v_refs[g][pl.ds(h, PAGE_SIZE, stride=N_HEADS), :] for h in range(N_HEADS)], axis=1)
         for g in range(npg)], axis=0).astype(BF16)
    update(s, v)

    @pl.when(p == n_steps - 1)
    def _():
        sn = jnp.dot(q2, ktn_ref[...].astype(BF16), preferred_element_type=F32)
        row = lax.broadcasted_iota(jnp.int32, sn.shape, 0)
        col = lax.broadcasted_iota(jnp.int32, sn.shape, 1)
        sn = jnp.where(col <= row % n_new, sn, -jnp.inf)
        vn = jnp.concatenate([vn_ref[...], jnp.zeros((PAGE_SIZE - n_new, D_MODEL), F32)], axis=0).astype(BF16)
        update(sn, vn)
        lam = _lambda(lp_ref, lambda_init)
        on = acc_s[...] / l_s[...]
        for h in range(N_HEADS):
            blk = on[h * 2 * n_new:(h + 1) * 2 * n_new, h * V_DIM:(h + 1) * V_DIM]
            o_ref[:, h * V_DIM:(h + 1) * V_DIM] = _head_out(blk[:n_new], blk[n_new:], lam, sg_ref[...],
                                                             lambda_init)


def _lru_kernel(u_ref, buf_ref, h0_ref, cw_ref, cb_ref, wa_ref, ba_ref, wi_ref, bi_ref, lam_ref,
                y_ref, hl_ref, ct_ref, xbuf, a_s, b_s, hc, *, ts):
    t = pl.program_id(1)

    @pl.when(t == 0)
    def _():
        xbuf[0:8, :] = buf_ref[...]
        hc[...] = jnp.broadcast_to(h0_ref[...], (8, D_RNN))

    xr = u_ref[:, D_RNN:]
    xbuf[8:8 + ts, :] = xr
    cw = cw_ref[...]
    xc = cb_ref[...] + (cw[0:1] * xbuf[5:5 + ts, :] + cw[1:2] * xbuf[6:6 + ts, :]
                        + cw[2:3] * xbuf[7:7 + ts, :] + cw[3:4] * xr)
    z = -lam_ref[...]
    sp = jnp.maximum(z, 0.0) + jnp.log1p(jnp.exp(-jnp.abs(z)))
    row = lax.broadcasted_iota(jnp.int32, (ts, RG_BLOCK), 0) % 8
    for n in range(N_RG_BLOCKS):
        sl = slice(n * RG_BLOCK, (n + 1) * RG_BLOCK)
        xcn = xc[:, sl]
        xb = xcn.astype(BF16)
        r = jax.nn.sigmoid(jnp.dot(xb, wa_ref[n], preferred_element_type=F32) + ba_ref[:, sl])
        ig = jax.nn.sigmoid(jnp.dot(xb, wi_ref[n], preferred_element_type=F32) + bi_ref[:, sl])
        log_a = -RG_C * r * sp[:, sl]
        a = jnp.exp(log_a)
        mult = jnp.sqrt(-jnp.tanh(log_a) * (a * a + 1.0))
        b = mult * (ig * xcn)
        for s in (1, 2, 4):
            a_sh = pltpu.roll(a, s, 0)
            b_sh = pltpu.roll(b, s, 0)
            keep = row >= s
            b = jnp.where(keep, a * b_sh + b, b)
            a = jnp.where(keep, a * a_sh, a)
        a_s[:, sl] = a
        b_s[:, sl] = b

    def step(i, h):
        r0 = pl.multiple_of(i * 8, 8)
        ht = a_s[pl.ds(r0, 8), :] * h + b_s[pl.ds(r0, 8), :]
        b_s[pl.ds(r0, 8), :] = ht
        return jnp.broadcast_to(ht[7:8, :], (8, D_RNN))

    h = lax.fori_loop(0, ts // 8, step, hc[...])
    hc[...] = h
    xbuf[0:8, :] = xbuf[ts:ts + 8, :]
    y_ref[...] = (b_s[...] * jax.nn.gelu(u_ref[:, :D_RNN])).astype(y_ref.dtype)

    @pl.when(t == pl.num_programs(1) - 1)
    def _():
        hl_ref[...] = h[0:1]
        ct_ref[...] = xbuf[0:8, :]


def _lru(u, buf0, h0, cw, cb, wa, ba, wi, bi, lam, n_seq, seq, ts, row_blk0, y_dtype):
    nt = seq // ts
    c2 = lambda b, t: (0, 0)
    c3 = lambda b, t: (0, 0, 0)
    return pl.pallas_call(
        functools.partial(_lru_kernel, ts=ts),
        grid=(n_seq, nt),
        in_specs=[
            pl.BlockSpec((ts, 2 * D_RNN), lambda b, t: (row_blk0 + b * nt + t, 0)),
            pl.BlockSpec((None, 8, D_RNN), lambda b, t: (b, 0, 0)),
            pl.BlockSpec((None, 1, D_RNN), lambda b, t: (b, 0, 0)),
            pl.BlockSpec((CONV_W, D_RNN), c2),
            pl.BlockSpec((1, D_RNN), c2),
            pl.BlockSpec((N_RG_BLOCKS, RG_BLOCK, RG_BLOCK), c3),
            pl.BlockSpec((1, D_RNN), c2),
            pl.BlockSpec((N_RG_BLOCKS, RG_BLOCK, RG_BLOCK), c3),
            pl.BlockSpec((1, D_RNN), c2),
            pl.BlockSpec((1, D_RNN), c2),
        ],
        out_specs=[
            pl.BlockSpec((ts, D_RNN), lambda b, t: (b * nt + t, 0)),
            pl.BlockSpec((None, 1, D_RNN), lambda b, t: (b, 0, 0)),
            pl.BlockSpec((None, 8, D_RNN), lambda b, t: (b, 0, 0)),
        ],
        out_shape=[
            jax.ShapeDtypeStruct((n_seq * seq, D_RNN), y_dtype),
            jax.ShapeDtypeStruct((n_seq, 1, D_RNN), F32),
            jax.ShapeDtypeStruct((n_seq, 8, D_RNN), F32),
        ],
        scratch_shapes=[pltpu.VMEM((ts + 8, D_RNN), F32), pltpu.VMEM((ts, D_RNN), F32),
                        pltpu.VMEM((ts, D_RNN), F32), pltpu.VMEM((8, D_RNN), F32)],
        compiler_params=_cparams(("arbitrary", "arbitrary")),
        name="lru",
    )(u, buf0, h0, cw, cb, wa, ba, wi, bi, lam)


def _dispatch(info, counts_row):
    t_all = info.shape[0]
    eid = info[:, _R_E1:_R_E2 + 1].astype(jnp.int32)
    rank = info[:, _R_RANK1:_R_RANK2 + 1].astype(jnp.int32)
    counts = counts_row[0, _EXPERT_LANE0:_EXPERT_LANE0 + N_EXPERTS].astype(jnp.int32)
    pcounts = ((counts + MOE_TM - 1) // MOE_TM) * MOE_TM
    pends = jnp.cumsum(pcounts)
    pstarts = pends - pcounts
    dest = pstarts.at[eid].get(mode='promise_in_bounds') + rank
    n_blk = (t_all * TOP_K) // MOE_TM + N_EXPERTS
    blk_row0 = jnp.arange(n_blk, dtype=jnp.int32) * MOE_TM
    blk_e = jnp.minimum(jnp.sum((pends[None, :] <= blk_row0[:, None]).astype(jnp.int32), axis=1), N_EXPERTS - 1)
    n_valid = (pends[-1] // MOE_TM).astype(jnp.int32).reshape(1)
    tok = jnp.broadcast_to(jnp.arange(t_all, dtype=jnp.int32)[:, None], (t_all, TOP_K))
    row_tok = (jnp.arange(n_blk * MOE_TM, dtype=jnp.int32) % t_all).at[dest.reshape(-1)].set(
        tok.reshape(-1), mode='promise_in_bounds', unique_indices=True)
    return dest, blk_e.astype(jnp.int32), n_valid, row_tok


def kernel(x_prompt, x_sample, cache_k, cache_v, state_h, state_conv, page_table, p_prompt, p_sample, mix_norm, ffn_norm, ple_norm, w_qkv, q_norm, k_norm, lambda_q1, lambda_k1, lambda_q2, lambda_k2, sub_norm, w_o, w_in_lru, conv_w, conv_b, w_rg_a, b_rg_a, w_rg_i, b_rg_i, lru_lambda, w_out_lru, w_group, b_group, w_sub, b_sub, w_gate_up, w_down, w_ple_gate, w_ple_proj):
    batch, seq, _ = x_prompt.shape
    n_req, n_new, _ = x_sample.shape
    n_pages = page_table.shape[1]
    past_len = n_pages * PAGE_SIZE
    t_p = batch * seq
    t_s = n_req * n_new
    assert t_s == TM and t_p % TM == 0 and seq % TM == 0 and n_pages % PAGES_PER_STEP == 0
    assert (t_p + t_s) % TM_WIDE == 0 and seq % TQ == 0
    n_p = t_p // TM
    n_attn = cache_k.shape[0]
    n_pool = cache_k.shape[1]

    h = jnp.concatenate([x_prompt.reshape(t_p, D_MODEL), x_sample.reshape(t_s, D_MODEL)], axis=0)

    pos = jnp.concatenate([jnp.tile(jnp.arange(seq, dtype=jnp.int32), batch),
                           jnp.tile(past_len + jnp.arange(n_new, dtype=jnp.int32), n_req)])
    half = ROPE_DIM // 2
    inv = ROPE_THETA ** (-(jnp.arange(half, dtype=F32) * 2.0 / ROPE_DIM))
    ang = pos.astype(F32)[:, None] * inv[None, :]
    cos_t = jnp.cos(ang).T
    sin_t = jnp.sin(ang).T

    ck_t = cache_k.transpose(0, 1, 3, 4, 5, 2).reshape(n_attn, n_pool, D_MODEL, PAGE_SIZE)
    cv_r = cache_v.reshape(n_attn, n_pool, PAGE_SIZE * N_HEADS, V_DIM)
    pt_flat = page_table.reshape(-1).astype(jnp.int32)
    pp = p_prompt.reshape(DEPTH, t_p, PLE_DIM)
    ps = p_sample.reshape(DEPTH, t_s, PLE_DIM)

    k_s, v_s, hl_p, ct_p, hl_s, ct_s = [], [], [], [], [], []
    y_p = y_s = kv_prev = None
    for i in range(DEPTH):
        j = i // 2
        g_mix = mix_norm[i].reshape(1, D_MODEL).astype(F32)
        if i % 2 == 0:
            lambda_init = 0.8 - 0.6 * math.exp(-0.3 * i)
            wq_t = w_qkv[j, :, :D_MODEL].T.astype(BF16)
            wk_t = w_qkv[j, :, D_MODEL:2 * D_MODEL].T.astype(BF16)
            wv = w_qkv[j, :, 2 * D_MODEL:].astype(BF16)
            qg_b = jnp.broadcast_to(q_norm[j].astype(F32)[:, None], (QK_DIM, TM))
            kg_b = jnp.broadcast_to(k_norm[j].astype(F32)[:, None], (QK_DIM, TM))
            qt_b, q_s, kt_p, kt_s, k_b, vv_p, vv_s, vt_b = _qkv(h, g_mix, wq_t, wk_t, wv, qg_b, kg_b, cos_t, sin_t,
                                                                n_p, batch, seq, j, n_attn, kv_prev)
            kv_prev = (kt_p, vv_p)
            lam_p = jnp.stack([lambda_q1[j], lambda_k1[j], lambda_q2[j], lambda_k2[j]]).astype(F32)
            sub_g = sub_norm[j].reshape(1, V_DIM).astype(F32)
            sub_g_b = jnp.broadcast_to(sub_norm[j].astype(F32)[:, None], (V_DIM, TQ))
            ktn = kt_s.reshape(D_MODEL, n_req, n_new).transpose(1, 0, 2)
            ktn = jnp.pad(ktn, ((0, 0), (0, 0), (0, PAGE_SIZE - n_new)))
            o_p, o_s = _attention(pt_flat, qt_b, k_b, vt_b, lam_p, sub_g_b, q_s, ktn, vv_s, sub_g, ck_t, cv_r, j,
                                  batch, seq, n_req, n_new, n_pages, lambda_init)
            h = _resid_matmul(h, o_p, o_s, w_o[j].astype(BF16), n_p)
            k_s.append(kt_s.T.reshape(n_req, n_new, N_HEADS, 2, QK_DIM))
            v_s.append(vv_s.reshape(n_req, n_new, N_HEADS, V_DIM))
        else:
            u = _norm_matmul(h, g_mix, w_in_lru[j].astype(BF16))
            cw = conv_w[j].astype(F32)
            cb = conv_b[j].reshape(1, D_RNN).astype(F32)
            wa = w_rg_a[j].astype(BF16)
            wi = w_rg_i[j].astype(BF16)
            ba = b_rg_a[j].reshape(1, D_RNN).astype(F32)
            bi = b_rg_i[j].reshape(1, D_RNN).astype(F32)
            lam = lru_lambda[j].reshape(1, D_RNN).astype(F32)
            zb = jnp.zeros((batch, 8, D_RNN), F32)
            zh = jnp.zeros((batch, 1, D_RNN), F32)
            yl_p, hlp, ctp = _lru(u, zb, zh, cw, cb, wa, ba, wi, bi, lam, batch, seq, LRU_TS, 0, BF16)
            buf_s = jnp.pad(state_conv[j].astype(F32), ((0, 0), (8 - (CONV_W - 1), 0), (0, 0)))
            h0_s = state_h[j].astype(F32)[:, None, :]
            yl_s, hls, cts = _lru(u, buf_s, h0_s, cw, cb, wa, ba, wi, bi, lam, n_req, n_new, n_new,
                                  t_p // n_new, F32)
            h = _resid_matmul(h, yl_p, yl_s, w_out_lru[j].astype(BF16), n_p)
            hl_p.append(hlp[:, 0])
            ct_p.append(ctp[:, 8 - (CONV_W - 1):])
            hl_s.append(hls[:, 0])
            ct_s.append(cts[:, 8 - (CONV_W - 1):])

        w_r = jnp.concatenate([w_group[i], w_sub[i]], axis=1).astype(F32)
        w_r = jnp.pad(w_r, ((0, 0), (0, 128 - w_r.shape[1])))
        w_hi = w_r.astype(BF16)
        w_lo = (w_r - w_hi.astype(F32)).astype(BF16)
        bias = jnp.concatenate([b_group[i], b_sub[i]]).astype(F32)
        bias = jnp.pad(bias, (0, _R_LANES - bias.shape[0])).reshape(1, _R_LANES)
        hn, info, counts = _router(h, ffn_norm[i].reshape(1, D_MODEL).astype(F32), w_hi, w_lo, bias)
        dest, blk_e, n_valid, row_tok = _dispatch(info, counts)
        xs = hn.at[row_tok].get(mode='promise_in_bounds')
        ys = _moe_ffn(blk_e, n_valid, xs, w_gate_up, w_down, i)
        ma = ys.at[dest[:, 0]].get(mode='promise_in_bounds')
        mb = ys.at[dest[:, 1]].get(mode='promise_in_bounds')

        last = i == DEPTH - 1
        out = _ple(h, ma, mb, info, ple_norm[i].reshape(1, D_MODEL).astype(F32), w_ple_gate[i].astype(BF16),
                   pp, ps, w_ple_proj[i].astype(BF16), i, n_p, last)
        if last:
            y_p, y_s = out
        else:
            h = out

    kt_all, v_all = kv_prev
    k_prompt = kt_all.reshape(n_attn, batch, N_HEADS, 2, QK_DIM, seq).transpose(0, 1, 5, 2, 3, 4)
    v_prompt = v_all.reshape(n_attn, batch, seq, N_HEADS, V_DIM)
    return (y_p.reshape(batch, seq, D_MODEL), y_s.reshape(n_req, n_new, D_MODEL),
            k_prompt, v_prompt, jnp.stack(hl_p), jnp.stack(ct_p),
            jnp.stack(k_s), jnp.stack(v_s), jnp.stack(hl_s), jnp.stack(ct_s))
```

```python
import functools
import math

import jax
import jax.numpy as jnp
import numpy as np
from jax import lax
from jax.experimental import pallas as pl
from jax.experimental.pallas import tpu as pltpu

F32 = jnp.float32
BF16 = jnp.bfloat16

D_MODEL = 1024
DEPTH = 4
PAGE_SIZE = 128
N_HEADS = 8
QK_DIM = 64
V_DIM = 128
ROPE_DIM = 16
ROPE_THETA = 500000.0
D_RNN = 1280
RG_BLOCK = 128
N_RG_BLOCKS = 10
CONV_W = 4
RG_C = 8.0
N_GROUPS = 4
EXPERTS_PER_GROUP = 8
N_EXPERTS = 32
TOP_K = 2
D_EXPERT = 512
PLE_DIM = 256
EPS = 1e-6

TM = 256
TM_WIDE = 640
TQ = 512
TK = 256
PAGES_PER_STEP = 8
MOE_TM = 512
LRU_TS = 256
VMEM_LIMIT = 56 * 1024 * 1024

_NT = (((1,), (1,)), ((), ()))
_STRIP = 128
_Q_SCALE = QK_DIM ** -0.5 * math.log2(math.e)


def _cparams(sem):
    return pltpu.CompilerParams(dimension_semantics=sem, vmem_limit_bytes=VMEM_LIMIT)


def _rms(x, g):
    return x * lax.rsqrt(jnp.mean(x * x, axis=-1, keepdims=True) + EPS) * g


def _tile(i, n_p):
    return jnp.where(i == 0, n_p, i - 1)


def _ptile(i):
    return jnp.maximum(i - 1, 0)


def _qkv_kernel(h_ref, g_ref, wq_ref, wk_ref, wv_ref, qg_ref, kg_ref, cos_ref, sin_ref, *rest):
    qtb_ref, qs_ref, ktp_ref, kts_ref, kb_ref, vp_ref, vs_ref, vtb_ref, qt_s, kt_s = rest[-10:]
    i = pl.program_id(0)
    hn = _rms(h_ref[...], g_ref[...]).astype(BF16)
    cos = cos_ref[...]
    sin = sin_ref[...]

    def normrope(w_ref, gain_ref, dst, scale):
        xt = lax.dot_general(w_ref[...], hn, _NT, preferred_element_type=F32)
        gain = gain_ref[...]
        for g in range(2 * N_HEADS):
            xg = xt[g * QK_DIM:(g + 1) * QK_DIM, :]
            ss = jnp.sum(xg * xg, axis=0, keepdims=True)
            yg = xg * lax.rsqrt(ss * (1.0 / QK_DIM) + EPS) * gain
            y1 = yg[0:8]
            y2 = yg[8:16]
            dst[g * QK_DIM:g * QK_DIM + 8, :] = (y1 * cos - y2 * sin) * scale
            dst[g * QK_DIM + 8:g * QK_DIM + 16, :] = (y2 * cos + y1 * sin) * scale
            dst[g * QK_DIM + 16:(g + 1) * QK_DIM, :] = yg[16:] * scale

    normrope(wq_ref, qg_ref, qt_s, _Q_SCALE)
    normrope(wk_ref, kg_ref, kt_s, 1.0)
    v = jnp.dot(hn, wv_ref[...], preferred_element_type=F32)

    @pl.when(i == 0)
    def _():
        qs_ref[...] = qt_s[...].T
        kts_ref[...] = kt_s[...]
        vs_ref[...] = v

    @pl.when(i > 0)
    def _():
        kt = kt_s[...]
        qtb_ref[...] = qt_s[...].astype(BF16)
        ktp_ref[...] = kt
        kb_ref[...] = kt.T.astype(BF16)
        vp_ref[...] = v
        vtb_ref[...] = v.T.astype(BF16)


def _qkv(h, g, wq_t, wk_t, wv, qg_b, kg_b, cos_t, sin_t, n_p, batch, seq, layer, n_attn, prev):
    n_sb = seq // TM
    const = lambda i: (0, 0)
    tile = lambda i: (_tile(i, n_p), 0)
    ktile = lambda i: (_ptile(i) // n_sb, 0, _ptile(i) % n_sb)
    ltile = lambda i: (layer, _ptile(i) // n_sb, 0, _ptile(i) % n_sb)
    prev = () if prev is None else tuple(prev)
    n_in = 9
    return pl.pallas_call(
        _qkv_kernel,
        grid=(n_p + 1,),
        in_specs=[
            pl.BlockSpec((TM, D_MODEL), tile),
            pl.BlockSpec((1, D_MODEL), const),
            pl.BlockSpec((D_MODEL, D_MODEL), const),
            pl.BlockSpec((D_MODEL, D_MODEL), const),
            pl.BlockSpec((D_MODEL, D_MODEL), const),
            pl.BlockSpec((QK_DIM, TM), const),
            pl.BlockSpec((QK_DIM, TM), const),
            pl.BlockSpec((8, TM), lambda i: (0, _tile(i, n_p))),
            pl.BlockSpec((8, TM), lambda i: (0, _tile(i, n_p))),
        ] + [pl.BlockSpec(memory_space=pl.ANY)] * len(prev),
        out_specs=[
            pl.BlockSpec((None, D_MODEL, TM), ktile),
            pl.BlockSpec((TM, D_MODEL), const),
            pl.BlockSpec((None, None, D_MODEL, TM), ltile),
            pl.BlockSpec((D_MODEL, TM), const),
            pl.BlockSpec((TM, D_MODEL), lambda i: (_ptile(i), 0)),
            pl.BlockSpec((None, TM, D_MODEL), lambda i: (layer, _ptile(i), 0)),
            pl.BlockSpec((TM, D_MODEL), const),
            pl.BlockSpec((None, D_MODEL, TM), ktile),
        ],
        out_shape=[
            jax.ShapeDtypeStruct((batch, D_MODEL, seq), BF16),
            jax.ShapeDtypeStruct((TM, D_MODEL), F32),
            jax.ShapeDtypeStruct((n_attn, batch, D_MODEL, seq), F32),
            jax.ShapeDtypeStruct((D_MODEL, TM), F32),
            jax.ShapeDtypeStruct((n_p * TM, D_MODEL), BF16),
            jax.ShapeDtypeStruct((n_attn, n_p * TM, D_MODEL), F32),
            jax.ShapeDtypeStruct((TM, D_MODEL), F32),
            jax.ShapeDtypeStruct((batch, D_MODEL, seq), BF16),
        ],
        input_output_aliases={n_in: 2, n_in + 1: 5} if prev else {},
        scratch_shapes=[pltpu.VMEM((D_MODEL, TM), F32), pltpu.VMEM((D_MODEL, TM), F32)],
        compiler_params=_cparams(("arbitrary",)),
        name="qkv",
    )(h, g, wq_t, wk_t, wv, qg_b, kg_b, cos_t, sin_t, *prev)


def _resid_matmul_kernel(h_ref, xp_ref, xs_ref, w_ref, o_ref):
    i = pl.program_id(0)
    x = jnp.where(i == 0, xs_ref[...].astype(BF16), xp_ref[...].astype(BF16))
    o_ref[...] = h_ref[...] + jnp.dot(x, w_ref[...], preferred_element_type=F32)


def _resid_matmul(h, x_p, x_s, w, n_p):
    t_all = h.shape[0]
    k = w.shape[0]
    tile = lambda i: (_tile(i, n_p), 0)
    return pl.pallas_call(
        _resid_matmul_kernel,
        grid=(n_p + 1,),
        in_specs=[
            pl.BlockSpec((TM, D_MODEL), tile),
            pl.BlockSpec((TM, k), lambda i: (_ptile(i), 0)),
            pl.BlockSpec((TM, k), lambda i: (0, 0)),
            pl.BlockSpec((k, D_MODEL), lambda i: (0, 0)),
        ],
        out_specs=pl.BlockSpec((TM, D_MODEL), tile),
        out_shape=jax.ShapeDtypeStruct((t_all, D_MODEL), F32),
        compiler_params=_cparams(("arbitrary",)),
        name="resid_matmul",
    )(h, x_p, x_s, w)


def _norm_matmul_kernel(h_ref, g_ref, w_ref, o_ref):
    hn = _rms(h_ref[...], g_ref[...]).astype(BF16)
    o_ref[...] = jnp.dot(hn, w_ref[...], preferred_element_type=F32)


def _norm_matmul(h, g, w):
    t_all = h.shape[0]
    n = w.shape[1]
    return pl.pallas_call(
        _norm_matmul_kernel,
        grid=(t_all // TM_WIDE,),
        in_specs=[
            pl.BlockSpec((TM_WIDE, D_MODEL), lambda i: (i, 0)),
            pl.BlockSpec((1, D_MODEL), lambda i: (0, 0)),
            pl.BlockSpec((D_MODEL, n), lambda i: (0, 0)),
        ],
        out_specs=pl.BlockSpec((TM_WIDE, n), lambda i: (i, 0)),
        out_shape=jax.ShapeDtypeStruct((t_all, n), F32),
        compiler_params=_cparams(("arbitrary",)),
        name="norm_matmul",
    )(h, g, w)


_R_E1, _R_E2, _R_W1, _R_W2, _R_RANK1, _R_RANK2 = range(6)
_R_LANES = 128
_R_ROWS = 8
_EXPERT_LANE0 = N_GROUPS


def _router_kernel(h_ref, g_ref, whi_ref, wlo_ref, bias_ref, hn_ref, info_ref, cnt_ref, cnt_s):
    i = pl.program_id(0)

    @pl.when(i == 0)
    def _():
        cnt_s[...] = jnp.zeros_like(cnt_s)

    hn = _rms(h_ref[...], g_ref[...])
    hi = hn.astype(BF16)
    bits = pltpu.bitcast(hi.astype(F32), jnp.uint32)
    hn_ref[...] = (bits[:, :D_MODEL // 2] >> 16) | bits[:, D_MODEL // 2:]
    lo = (hn - hi.astype(F32)).astype(BF16)
    whi = whi_ref[...]
    lg = jnp.dot(hi, whi, preferred_element_type=F32)
    lg += jnp.dot(lo, whi, preferred_element_type=F32)
    lg += jnp.dot(hi, wlo_ref[...], preferred_element_type=F32)
    lg += bias_ref[...]

    lane = lax.broadcasted_iota(jnp.int32, lg.shape, 1).astype(F32)
    far = float(_R_LANES)
    rmax = lambda x: jnp.max(x, axis=-1, keepdims=True)
    rmin = lambda x: jnp.min(x, axis=-1, keepdims=True)
    rsum = lambda x: jnp.sum(x, axis=-1, keepdims=True)

    gl = jnp.where(lane < N_GROUPS, lg, -jnp.inf)
    gmax = rmax(gl)
    ge = jnp.exp(gl - gmax)
    gp = ge / rsum(ge)
    g_idx = rmin(jnp.where(gl == gmax, lane, far))
    g_w = rsum(jnp.where(lane == g_idx, gp, 0.0))

    lane0 = _EXPERT_LANE0 + g_idx * EXPERTS_PER_GROUP
    smask = (lane >= lane0) & (lane < lane0 + EXPERTS_PER_GROUP)
    sl = jnp.where(smask, lg, -jnp.inf)
    se = jnp.exp(sl - rmax(sl))
    sp = jnp.where(smask, se / rsum(se), -1.0)
    v1 = rmax(sp)
    i1 = rmin(jnp.where(sp == v1, lane, far))
    sp2 = jnp.where(lane == i1, -1.0, sp)
    v2 = rmax(sp2)
    i2 = rmin(jnp.where(sp2 == v2, lane, far))
    den = v1 + v2
    w1 = g_w * (v1 / den)
    w2 = g_w * (v2 / den)

    oh1 = jnp.where(lane == i1, 1.0, 0.0)
    oh2 = jnp.where(lane == i2, 1.0, 0.0)
    r = lax.broadcasted_iota(jnp.int32, (TM_WIDE, TM_WIDE), 0)
    c = lax.broadcasted_iota(jnp.int32, (TM_WIDE, TM_WIDE), 1)
    tri = jnp.where(c < r, 1.0, 0.0).astype(BF16)
    pre1 = jnp.dot(tri, oh1.astype(BF16), preferred_element_type=F32)
    pre2 = jnp.dot(tri, oh2.astype(BF16), preferred_element_type=F32)
    c1 = jnp.sum(oh1, axis=0, keepdims=True)
    c2 = jnp.sum(oh2, axis=0, keepdims=True)
    base = cnt_s[...]
    rank1 = rsum(oh1 * (pre1 + base))
    rank2 = rsum(oh2 * (pre2 + (base + c1)))
    total = base + (c1 + c2)
    cnt_s[...] = total
    cnt_ref[...] = total

    info = jnp.zeros_like(lg)
    for k, val in ((_R_E1, i1 - _EXPERT_LANE0), (_R_E2, i2 - _EXPERT_LANE0), (_R_W1, w1), (_R_W2, w2),
                   (_R_RANK1, rank1), (_R_RANK2, rank2)):
        info = jnp.where(lane == float(k), val, info)
    info_ref[...] = info.T[:_R_ROWS]


def _router(h, g, w_hi, w_lo, bias):
    t_all = h.shape[0]
    const = lambda i: (0, 0)
    return pl.pallas_call(
        _router_kernel,
        grid=(t_all // TM_WIDE,),
        in_specs=[
            pl.BlockSpec((TM_WIDE, D_MODEL), lambda i: (i, 0)),
            pl.BlockSpec((1, D_MODEL), const),
            pl.BlockSpec((D_MODEL, _R_LANES), const),
            pl.BlockSpec((D_MODEL, _R_LANES), const),
            pl.BlockSpec((1, _R_LANES), const),
        ],
        out_specs=[
            pl.BlockSpec((TM_WIDE, D_MODEL // 2), lambda i: (i, 0)),
            pl.BlockSpec((_R_ROWS, TM_WIDE), lambda i: (0, i)),
            pl.BlockSpec((1, _R_LANES), const),
        ],
        out_shape=[
            jax.ShapeDtypeStruct((t_all, D_MODEL // 2), jnp.uint32),
            jax.ShapeDtypeStruct((_R_ROWS, t_all), F32),
            jax.ShapeDtypeStruct((1, _R_LANES), F32),
        ],
        scratch_shapes=[pltpu.VMEM((1, _R_LANES), F32)],
        compiler_params=_cparams(("arbitrary",)),
        name="router",
    )(h, g, w_hi, w_lo, bias)


def _moe_ffn_kernel(be_ref, nv_ref, x_ref, wgu_ref, wd_ref, o_ref, wgu_s, wd_s):
    j = pl.program_id(0)
    valid = j < nv_ref[0]
    new_expert = (j == 0) | (be_ref[j] != be_ref[jnp.maximum(j - 1, 0)])

    @pl.when(valid & new_expert)
    def _():
        wgu_s[...] = wgu_ref[...].astype(BF16)
        wd_s[...] = wd_ref[...].astype(BF16)

    @pl.when(valid)
    def _():
        words = x_ref[...]
        x = jnp.concatenate([pltpu.bitcast(words << 16, F32),
                             pltpu.bitcast(words & jnp.uint32(0xFFFF0000), F32)], axis=1).astype(BF16)
        gu = jnp.dot(x, wgu_s[...], preferred_element_type=F32)
        act = jax.nn.silu(gu[:, :D_EXPERT]) * gu[:, D_EXPERT:]
        o_ref[...] = jnp.dot(act.astype(BF16), wd_s[...], preferred_element_type=F32)

    @pl.when(jnp.logical_not(valid))
    def _():
        o_ref[...] = jnp.zeros_like(o_ref)


def _moe_ffn(blk_e, n_valid, xs, wgu, wd, layer):
    n_rows = xs.shape[0]
    n_blk = n_rows // MOE_TM
    return pl.pallas_call(
        _moe_ffn_kernel,
        grid_spec=pltpu.PrefetchScalarGridSpec(
            num_scalar_prefetch=2,
            grid=(n_blk,),
            in_specs=[
                pl.BlockSpec((MOE_TM, D_MODEL // 2), lambda j, be, nv: (j, 0)),
                pl.BlockSpec((None, None, D_MODEL, 2 * D_EXPERT), lambda j, be, nv: (layer, be[j], 0, 0)),
                pl.BlockSpec((None, None, D_EXPERT, D_MODEL), lambda j, be, nv: (layer, be[j], 0, 0)),
            ],
            out_specs=pl.BlockSpec((MOE_TM, D_MODEL), lambda j, be, nv: (j, 0)),
            scratch_shapes=[pltpu.VMEM((D_MODEL, 2 * D_EXPERT), BF16), pltpu.VMEM((D_EXPERT, D_MODEL), BF16)],
        ),
        out_shape=jax.ShapeDtypeStruct((n_rows, D_MODEL), F32),
        compiler_params=_cparams(("arbitrary",)),
        name="moe_ffn",
    )(blk_e, n_valid, xs, wgu, wd)


def _ple_kernel(h_ref, ma_ref, mb_ref, info_ref, g_ref, wg_ref, pp_ref, ps_ref, wp_ref, *o_refs, split):
    i = pl.program_id(0)
    info = jnp.concatenate([info_ref[...], jnp.zeros((_R_LANES - _R_ROWS, TM), F32)], axis=0).T
    w1 = info[:, _R_W1:_R_W1 + 1]
    w2 = info[:, _R_W2:_R_W2 + 1]
    x = h_ref[...] + (w1 * ma_ref[...] + w2 * mb_ref[...])
    hn = _rms(x, g_ref[...]).astype(BF16)
    gate = jax.nn.sigmoid(jnp.dot(hn, wg_ref[...], preferred_element_type=F32))
    p = jnp.where(i == 0, ps_ref[...], pp_ref[...]).astype(BF16)
    out = x + gate * jnp.dot(p, wp_ref[...], preferred_element_type=F32)
    if split:
        yp_ref, ys_ref = o_refs

        @pl.when(i == 0)
        def _():
            ys_ref[...] = out

        @pl.when(i > 0)
        def _():
            yp_ref[...] = out
    else:
        o_refs[0][...] = out


def _ple(h, ma, mb, info, g, wg, p_p, p_s, wp, layer, n_p, split):
    t_all = h.shape[0]
    tile = lambda i: (_tile(i, n_p), 0)
    const = lambda i: (0, 0)
    if split:
        out_specs = [pl.BlockSpec((TM, D_MODEL), lambda i: (_ptile(i), 0)),
                     pl.BlockSpec((TM, D_MODEL), const)]
        out_shape = [jax.ShapeDtypeStruct((n_p * TM, D_MODEL), F32),
                     jax.ShapeDtypeStruct((TM, D_MODEL), F32)]
    else:
        out_specs = pl.BlockSpec((TM, D_MODEL), tile)
        out_shape = jax.ShapeDtypeStruct((t_all, D_MODEL), F32)
    return pl.pallas_call(
        functools.partial(_ple_kernel, split=split),
        grid=(n_p + 1,),
        in_specs=[
            pl.BlockSpec((TM, D_MODEL), tile),
            pl.BlockSpec((TM, D_MODEL), tile),
            pl.BlockSpec((TM, D_MODEL), tile),
            pl.BlockSpec((_R_ROWS, TM), lambda i: (0, _tile(i, n_p))),
            pl.BlockSpec((1, D_MODEL), const),
            pl.BlockSpec((D_MODEL, D_MODEL), const),
            pl.BlockSpec((None, TM, PLE_DIM), lambda i: (layer, _ptile(i), 0)),
            pl.BlockSpec((None, TM, PLE_DIM), lambda i: (layer, 0, 0)),
            pl.BlockSpec((PLE_DIM, D_MODEL), const),
        ],
        out_specs=out_specs,
        out_shape=out_shape,
        compiler_params=_cparams(("arbitrary",)),
        name="ple",
    )(h, ma, mb, info, g, wg, p_p, p_s, wp)


def _lambda(lp_ref, lambda_init):
    lp = lp_ref[...]
    s1 = jnp.sum(lp[0:1] * lp[1:2], axis=-1, keepdims=True)
    s2 = jnp.sum(lp[2:3] * lp[3:4], axis=-1, keepdims=True)
    return jnp.exp(s1) - jnp.exp(s2) + lambda_init


def _head_out(o0, o1, lam, sg, lambda_init):
    o = o0 - lam * o1
    return _rms(o, sg) * (1.0 - lambda_init)


def _flash_body(qi, qt_ref, k_ref, vt_ref, lp_ref, sg_ref, o_ref, acc_s, lambda_init):
    qt = qt_ref[...]
    zero = jnp.zeros((QK_DIM, TQ), qt.dtype)
    q2t = jnp.concatenate([jnp.concatenate([qt[:QK_DIM], zero], axis=0),
                           jnp.concatenate([zero, qt[QK_DIM:]], axis=0)], axis=1)
    acc_s[...] = jnp.zeros_like(acc_s)
    n_strip = 2 * TQ // _STRIP
    tiles_per_q = TQ // TK

    def attend(j, carry, diag):
        c0 = pl.multiple_of(j * TK, TK)
        kj = k_ref[pl.ds(c0, TK), :]
        vtj = vt_ref[:, pl.ds(c0, TK)]
        out = []
        for c in range(n_strip):
            cs = slice(c * _STRIP, (c + 1) * _STRIP)
            tok0 = (c * _STRIP) % TQ
            key0 = 0 if diag is None else diag * TK
            if diag is not None and key0 > tok0 + _STRIP - 1:
                out.append(carry[c])
                continue
            m_prev, l_prev = carry[c]
            st = jnp.dot(kj, q2t[:, cs], preferred_element_type=F32)
            if diag is not None and key0 + TK - 1 > tok0:
                key = lax.broadcasted_iota(jnp.int32, st.shape, 0) + key0
                tok = lax.broadcasted_iota(jnp.int32, st.shape, 1) + tok0
                st = jnp.where(key <= tok, st, -jnp.inf)
            m_new = jnp.maximum(m_prev, jnp.max(st, axis=0, keepdims=True))
            alpha = jnp.exp2(m_prev - m_new)
            p = jnp.exp2(st - m_new)
            l_new = alpha * l_prev + jnp.sum(p, axis=0, keepdims=True)
            pv = jnp.dot(vtj, p.astype(BF16), preferred_element_type=F32)
            acc_s[:, cs] = alpha * acc_s[:, cs] + pv
            out.append((m_new, l_new))
        return tuple(out)

    def group(jj, carry):
        for u in range(tiles_per_q):
            carry = attend(tiles_per_q * jj + u, carry, None)
        return carry

    init = tuple((jnp.full((1, _STRIP), -jnp.inf, F32), jnp.zeros((1, _STRIP), F32)) for _ in range(n_strip))
    carry = lax.fori_loop(0, qi, group, init)
    for d in range(tiles_per_q):
        carry = attend(tiles_per_q * qi + d, carry, d)

    on = acc_s[...] * (1.0 / jnp.concatenate([l for _, l in carry], axis=1))
    lam = _lambda(lp_ref, lambda_init)
    ot = on[:, :TQ] - lam * on[:, TQ:]
    y = ot * lax.rsqrt(jnp.mean(ot * ot, axis=0, keepdims=True) + EPS) * sg_ref[...]
    o_ref[...] = (y * (1.0 - lambda_init)).T.astype(o_ref.dtype)


def _attn_kernel(pt_ref, qt_ref, k_ref, vt_ref, lp_ref, sgb_ref, q_ref, ktn_ref, vn_ref, sg_ref, *rest,
                 lambda_init, n_new):
    del pt_ref
    npg = PAGES_PER_STEP
    k_refs = rest[:npg]
    v_refs = rest[npg:2 * npg]
    op_ref, os_ref = rest[2 * npg:2 * npg + 2]
    acc_p, q2_s, m_s, l_s, acc_s = rest[2 * npg + 2:]
    step = pl.program_id(2)
    _decode_body(step, pl.num_programs(2), q_ref, ktn_ref, vn_ref, lp_ref, sg_ref, k_refs, v_refs, os_ref,
                 q2_s, m_s, l_s, acc_s, lambda_init, n_new)
    _flash_body(step, qt_ref, k_ref, vt_ref, lp_ref, sgb_ref, op_ref, acc_p, lambda_init)


def _attention(pt_flat, qt_b, k_b, vt_b, lam_p, sub_g_b, q_s, ktn, v_s, sub_g, ck_t, cv_r, layer,
               batch, seq, n_req, n_new, n_pages, lambda_init):
    npg = PAGES_PER_STEP
    n_q = seq // TQ
    assert n_req == batch * N_HEADS and n_pages == n_q * npg
    n_cols = 2 * N_HEADS * n_new
    const = lambda b, h, i, pt: (0, 0)
    req = lambda b, h: b * N_HEADS + h

    def page_spec(g):
        return pl.BlockSpec((None, None, D_MODEL, PAGE_SIZE),
                            lambda b, h, i, pt: (layer, pt[req(b, h) * n_pages + i * npg + g], 0, 0))

    return pl.pallas_call(
        functools.partial(_attn_kernel, lambda_init=lambda_init, n_new=n_new),
        grid_spec=pltpu.PrefetchScalarGridSpec(
            num_scalar_prefetch=1,
            grid=(batch, N_HEADS, n_q),
            in_specs=[
                pl.BlockSpec((None, V_DIM, TQ), lambda b, h, i, pt: (b, h, i)),
                pl.BlockSpec((seq, V_DIM), lambda b, h, i, pt: (b, h)),
                pl.BlockSpec((None, V_DIM, seq), lambda b, h, i, pt: (b, h, 0)),
                pl.BlockSpec((4, QK_DIM), const),
                pl.BlockSpec((V_DIM, TQ), const),
                pl.BlockSpec((n_new, D_MODEL), lambda b, h, i, pt: (req(b, h), 0)),
                pl.BlockSpec((None, D_MODEL, PAGE_SIZE), lambda b, h, i, pt: (req(b, h), 0, 0)),
                pl.BlockSpec((n_new, D_MODEL), lambda b, h, i, pt: (req(b, h), 0)),
                pl.BlockSpec((1, V_DIM), const),
            ] + [page_spec(g) for g in range(npg)] + [page_spec(g) for g in range(npg)],
            out_specs=[
                pl.BlockSpec((TQ, V_DIM), lambda b, h, i, pt: (b * n_q + i, h)),
                pl.BlockSpec((n_new, D_MODEL), lambda b, h, i, pt: (req(b, h), 0)),
            ],
            scratch_shapes=[pltpu.VMEM((V_DIM, 2 * TQ), F32),
                            pltpu.VMEM((n_cols, D_MODEL), BF16), pltpu.VMEM((n_cols, 1), F32),
                            pltpu.VMEM((n_cols, 1), F32), pltpu.VMEM((n_cols, D_MODEL), F32)],
        ),
        out_shape=[jax.ShapeDtypeStruct((batch * seq, D_MODEL), BF16),
                   jax.ShapeDtypeStruct((n_req * n_new, D_MODEL), F32)],
        compiler_params=_cparams(("arbitrary", "arbitrary", "arbitrary")),
        name="attention",
    )(pt_flat, qt_b, k_b, vt_b, lam_p, sub_g_b, q_s, ktn, v_s, sub_g, *([ck_t] * npg), *([cv_r] * npg))


def _decode_body(p, n_steps, q_ref, ktn_ref, vn_ref, lp_ref, sg_ref, k_refs, v_refs, o_ref,
                 q2_s, m_s, l_s, acc_s, lambda_init, n_new):
    npg = len(k_refs)

    @pl.when(p == 0)
    def _():
        qq = jnp.concatenate([q_ref[...]] * (2 * N_HEADS), axis=0)
        row = lax.broadcasted_iota(jnp.int32, qq.shape, 0)
        lane = lax.broadcasted_iota(jnp.int32, qq.shape, 1)
        q2_s[...] = jnp.where(lane // QK_DIM == row // n_new, qq, 0.0).astype(BF16)
        m_s[...] = jnp.full_like(m_s, -jnp.inf)
        l_s[...] = jnp.zeros_like(l_s)
        acc_s[...] = jnp.zeros_like(acc_s)

    def update(s, v):
        m_prev = m_s[...]
        m_new = jnp.maximum(m_prev, jnp.max(s, axis=-1, keepdims=True))
        alpha = jnp.exp2(m_prev - m_new)
        pr = jnp.exp2(s - m_new)
        l_s[...] = alpha * l_s[...] + jnp.sum(pr, axis=-1, keepdims=True)
        acc_s[...] = alpha * acc_s[...] + jnp.dot(pr.astype(BF16), v, preferred_element_type=F32)
        m_s[...] = m_new

    q2 = q2_s[...]
    s = jnp.concatenate(
        [jnp.dot(q2, k_refs[g][...].astype(BF16), preferred_element_type=F32) for g in range(npg)], axis=1)
    v = jnp.concatenate(
        [jnp.concatenate([v_refs[g][pl.ds(h, PAGE_SIZE, stride=N_HEADS), :] for h in range(N_HEADS)], axis=1)
         for g in range(npg)], axis=0).astype(BF16)
    update(s, v)

    @pl.when(p == n_steps - 1)
    def _():
        sn = jnp.dot(q2, ktn_ref[...].astype(BF16), preferred_element_type=F32)
        row = lax.broadcasted_iota(jnp.int32, sn.shape, 0)
        col = lax.broadcasted_iota(jnp.int32, sn.shape, 1)
        sn = jnp.where(col <= row % n_new, sn, -jnp.inf)
        vn = jnp.concatenate([vn_ref[...], jnp.zeros((PAGE_SIZE - n_new, D_MODEL), F32)], axis=0).astype(BF16)
        update(sn, vn)
        lam = _lambda(lp_ref, lambda_init)
        on = acc_s[...] / l_s[...]
        for h in range(N_HEADS):
            blk = on[h * 2 * n_new:(h + 1) * 2 * n_new, h * V_DIM:(h + 1) * V_DIM]
            o_ref[:, h * V_DIM:(h + 1) * V_DIM] = _head_out(blk[:n_new], blk[n_new:], lam, sg_ref[...],
                                                             lambda_init)


def _lru_kernel(u_ref, buf_ref, h0_ref, cw_ref, cb_ref, wa_ref, ba_ref, wi_ref, bi_ref, lam_ref,
                y_ref, hl_ref, ct_ref, xbuf, a_s, b_s, hc, *, ts):
    t = pl.program_id(1)

    @pl.when(t == 0)
    def _():
        xbuf[0:8, :] = buf_ref[...]
        hc[...] = jnp.broadcast_to(h0_ref[...], (8, D_RNN))

    xr = u_ref[:, D_RNN:]
    xbuf[8:8 + ts, :] = xr
    cw = cw_ref[...]
    xc = cb_ref[...] + (cw[0:1] * xbuf[5:5 + ts, :] + cw[1:2] * xbuf[6:6 + ts, :]
                        + cw[2:3] * xbuf[7:7 + ts, :] + cw[3:4] * xr)
    z = -lam_ref[...]
    sp = jnp.maximum(z, 0.0) + jnp.log1p(jnp.exp(-jnp.abs(z)))
    row = lax.broadcasted_iota(jnp.int32, (ts, RG_BLOCK), 0) % 8
    for n in range(N_RG_BLOCKS):
        sl = slice(n * RG_BLOCK, (n + 1) * RG_BLOCK)
        xcn = xc[:, sl]
        xb = xcn.astype(BF16)
        r = jax.nn.sigmoid(jnp.dot(xb, wa_ref[n], preferred_element_type=F32) + ba_ref[:, sl])
        ig = jax.nn.sigmoid(jnp.dot(xb, wi_ref[n], preferred_element_type=F32) + bi_ref[:, sl])
        log_a = -RG_C * r * sp[:, sl]
        a = jnp.exp(log_a)
        mult = jnp.sqrt(-jnp.tanh(log_a) * (a * a + 1.0))
        b = mult * (ig * xcn)
        for s in (1, 2, 4):
            a_sh = pltpu.roll(a, s, 0)
            b_sh = pltpu.roll(b, s, 0)
            keep = row >= s
            b = jnp.where(keep, a * b_sh + b, b)
            a = jnp.where(keep, a * a_sh, a)
        a_s[:, sl] = a
        b_s[:, sl] = b

    def step(i, h):
        r0 = pl.multiple_of(i * 8, 8)
        ht = a_s[pl.ds(r0, 8), :] * h + b_s[pl.ds(r0, 8), :]
        b_s[pl.ds(r0, 8), :] = ht
        return jnp.broadcast_to(ht[7:8, :], (8, D_RNN))

    h = lax.fori_loop(0, ts // 8, step, hc[...])
    hc[...] = h
    xbuf[0:8, :] = xbuf[ts:ts + 8, :]
    y_ref[...] = (b_s[...] * jax.nn.gelu(u_ref[:, :D_RNN])).astype(y_ref.dtype)

    @pl.when(t == pl.num_programs(1) - 1)
    def _():
        hl_ref[...] = h[0:1]
        ct_ref[...] = xbuf[0:8, :]


def _lru(u, buf0, h0, cw, cb, wa, ba, wi, bi, lam, n_seq, seq, ts, row_blk0, y_dtype):
    nt = seq // ts
    c2 = lambda b, t: (0, 0)
    c3 = lambda b, t: (0, 0, 0)
    return pl.pallas_call(
        functools.partial(_lru_kernel, ts=ts),
        grid=(n_seq, nt),
        in_specs=[
            pl.BlockSpec((ts, 2 * D_RNN), lambda b, t: (row_blk0 + b * nt + t, 0)),
            pl.BlockSpec((None, 8, D_RNN), lambda b, t: (b, 0, 0)),
            pl.BlockSpec((None, 1, D_RNN), lambda b, t: (b, 0, 0)),
            pl.BlockSpec((CONV_W, D_RNN), c2),
            pl.BlockSpec((1, D_RNN), c2),
            pl.BlockSpec((N_RG_BLOCKS, RG_BLOCK, RG_BLOCK), c3),
            pl.BlockSpec((1, D_RNN), c2),
            pl.BlockSpec((N_RG_BLOCKS, RG_BLOCK, RG_BLOCK), c3),
            pl.BlockSpec((1, D_RNN), c2),
            pl.BlockSpec((1, D_RNN), c2),
        ],
        out_specs=[
            pl.BlockSpec((ts, D_RNN), lambda b, t: (b * nt + t, 0)),
            pl.BlockSpec((None, 1, D_RNN), lambda b, t: (b, 0, 0)),
            pl.BlockSpec((None, 8, D_RNN), lambda b, t: (b, 0, 0)),
        ],
        out_shape=[
            jax.ShapeDtypeStruct((n_seq * seq, D_RNN), y_dtype),
            jax.ShapeDtypeStruct((n_seq, 1, D_RNN), F32),
            jax.ShapeDtypeStruct((n_seq, 8, D_RNN), F32),
        ],
        scratch_shapes=[pltpu.VMEM((ts + 8, D_RNN), F32), pltpu.VMEM((ts, D_RNN), F32),
                        pltpu.VMEM((ts, D_RNN), F32), pltpu.VMEM((8, D_RNN), F32)],
        compiler_params=_cparams(("arbitrary", "arbitrary")),
        name="lru",
    )(u, buf0, h0, cw, cb, wa, ba, wi, bi, lam)


def _dispatch(info, counts_row):
    t_all = info.shape[1]
    eid = info[_R_E1:_R_E2 + 1].astype(jnp.int32)
    rank = info[_R_RANK1:_R_RANK2 + 1].astype(jnp.int32)
    counts = counts_row[0, _EXPERT_LANE0:_EXPERT_LANE0 + N_EXPERTS].astype(jnp.int32)
    pcounts = ((counts + MOE_TM - 1) // MOE_TM) * MOE_TM
    pends = jnp.cumsum(pcounts)
    pstarts = pends - pcounts
    dest = pstarts.at[eid].get(mode='promise_in_bounds') + rank
    n_blk = (t_all * TOP_K) // MOE_TM + N_EXPERTS
    blk_row0 = jnp.arange(n_blk, dtype=jnp.int32) * MOE_TM
    blk_e = jnp.minimum(jnp.sum((pends[None, :] <= blk_row0[:, None]).astype(jnp.int32), axis=1), N_EXPERTS - 1)
    n_valid = (pends[-1] // MOE_TM).astype(jnp.int32).reshape(1)
    tok = jnp.broadcast_to(jnp.arange(t_all, dtype=jnp.int32)[None, :], (TOP_K, t_all))
    row_tok = (jnp.arange(n_blk * MOE_TM, dtype=jnp.int32) % t_all).at[dest.reshape(-1)].set(
        tok.reshape(-1), mode='promise_in_bounds', unique_indices=True)
    return dest, blk_e.astype(jnp.int32), n_valid, row_tok


def kernel(x_prompt, x_sample, cache_k, cache_v, state_h, state_conv, page_table, p_prompt, p_sample, mix_norm, ffn_norm, ple_norm, w_qkv, q_norm, k_norm, lambda_q1, lambda_k1, lambda_q2, lambda_k2, sub_norm, w_o, w_in_lru, conv_w, conv_b, w_rg_a, b_rg_a, w_rg_i, b_rg_i, lru_lambda, w_out_lru, w_group, b_group, w_sub, b_sub, w_gate_up, w_down, w_ple_gate, w_ple_proj):
    batch, seq, _ = x_prompt.shape
    n_req, n_new, _ = x_sample.shape
    n_pages = page_table.shape[1]
    past_len = n_pages * PAGE_SIZE
    t_p = batch * seq
    t_s = n_req * n_new
    assert t_s == TM and t_p % TM == 0 and seq % TM == 0 and n_pages % PAGES_PER_STEP == 0
    assert (t_p + t_s) % TM_WIDE == 0 and seq % TQ == 0
    n_p = t_p // TM
    n_attn = cache_k.shape[0]
    n_pool = cache_k.shape[1]

    h = jnp.concatenate([x_prompt.reshape(t_p, D_MODEL), x_sample.reshape(t_s, D_MODEL)], axis=0)

    pos = jnp.concatenate([jnp.tile(jnp.arange(seq, dtype=jnp.int32), batch),
                           jnp.tile(past_len + jnp.arange(n_new, dtype=jnp.int32), n_req)])
    half = ROPE_DIM // 2
    inv = ROPE_THETA ** (-(jnp.arange(half, dtype=F32) * 2.0 / ROPE_DIM))
    ang = pos.astype(F32)[:, None] * inv[None, :]
    cos_t = jnp.cos(ang).T
    sin_t = jnp.sin(ang).T

    ck_t = cache_k.transpose(0, 1, 3, 4, 5, 2).reshape(n_attn, n_pool, D_MODEL, PAGE_SIZE)
    cv_r = cache_v.reshape(n_attn, n_pool, PAGE_SIZE * N_HEADS, V_DIM)
    pt_flat = page_table.reshape(-1).astype(jnp.int32)
    pp = p_prompt.reshape(DEPTH, t_p, PLE_DIM)
    ps = p_sample.reshape(DEPTH, t_s, PLE_DIM)

    k_s, v_s, hl_p, ct_p, hl_s, ct_s = [], [], [], [], [], []
    y_p = y_s = kv_prev = None
    for i in range(DEPTH):
        j = i // 2
        g_mix = mix_norm[i].reshape(1, D_MODEL).astype(F32)
        if i % 2 == 0:
            lambda_init = 0.8 - 0.6 * math.exp(-0.3 * i)
            wq_t = w_qkv[j, :, :D_MODEL].T.astype(BF16)
            wk_t = w_qkv[j, :, D_MODEL:2 * D_MODEL].T.astype(BF16)
            wv = w_qkv[j, :, 2 * D_MODEL:].astype(BF16)
            qg_b = jnp.broadcast_to(q_norm[j].astype(F32)[:, None], (QK_DIM, TM))
            kg_b = jnp.broadcast_to(k_norm[j].astype(F32)[:, None], (QK_DIM, TM))
            qt_b, q_s, kt_p, kt_s, k_b, vv_p, vv_s, vt_b = _qkv(h, g_mix, wq_t, wk_t, wv, qg_b, kg_b, cos_t, sin_t,
                                                                n_p, batch, seq, j, n_attn, kv_prev)
            kv_prev = (kt_p, vv_p)
            lam_p = jnp.stack([lambda_q1[j], lambda_k1[j], lambda_q2[j], lambda_k2[j]]).astype(F32)
            sub_g = sub_norm[j].reshape(1, V_DIM).astype(F32)
            sub_g_b = jnp.broadcast_to(sub_norm[j].astype(F32)[:, None], (V_DIM, TQ))
            ktn = kt_s.reshape(D_MODEL, n_req, n_new).transpose(1, 0, 2)
            ktn = jnp.pad(ktn, ((0, 0), (0, 0), (0, PAGE_SIZE - n_new)))
            o_p, o_s = _attention(pt_flat, qt_b, k_b, vt_b, lam_p, sub_g_b, q_s, ktn, vv_s, sub_g, ck_t, cv_r, j,
                                  batch, seq, n_req, n_new, n_pages, lambda_init)
            h = _resid_matmul(h, o_p, o_s, w_o[j].astype(BF16), n_p)
            k_s.append(kt_s.T.reshape(n_req, n_new, N_HEADS, 2, QK_DIM))
            v_s.append(vv_s.reshape(n_req, n_new, N_HEADS, V_DIM))
        else:
            u = _norm_matmul(h, g_mix, w_in_lru[j].astype(BF16))
            cw = conv_w[j].astype(F32)
            cb = conv_b[j].reshape(1, D_RNN).astype(F32)
            wa = w_rg_a[j].astype(BF16)
            wi = w_rg_i[j].astype(BF16)
            ba = b_rg_a[j].reshape(1, D_RNN).astype(F32)
            bi = b_rg_i[j].reshape(1, D_RNN).astype(F32)
            lam = lru_lambda[j].reshape(1, D_RNN).astype(F32)
            zb = jnp.zeros((batch, 8, D_RNN), F32)
            zh = jnp.zeros((batch, 1, D_RNN), F32)
            yl_p, hlp, ctp = _lru(u, zb, zh, cw, cb, wa, ba, wi, bi, lam, batch, seq, LRU_TS, 0, BF16)
            buf_s = jnp.pad(state_conv[j].astype(F32), ((0, 0), (8 - (CONV_W - 1), 0), (0, 0)))
            h0_s = state_h[j].astype(F32)[:, None, :]
            yl_s, hls, cts = _lru(u, buf_s, h0_s, cw, cb, wa, ba, wi, bi, lam, n_req, n_new, n_new,
                                  t_p // n_new, F32)
            h = _resid_matmul(h, yl_p, yl_s, w_out_lru[j].astype(BF16), n_p)
            hl_p.append(hlp[:, 0])
            ct_p.append(ctp[:, 8 - (CONV_W - 1):])
            hl_s.append(hls[:, 0])
            ct_s.append(cts[:, 8 - (CONV_W - 1):])

        w_r = jnp.concatenate([w_group[i], w_sub[i]], axis=1).astype(F32)
        w_r = jnp.pad(w_r, ((0, 0), (0, 128 - w_r.shape[1])))
        w_hi = w_r.astype(BF16)
        w_lo = (w_r - w_hi.astype(F32)).astype(BF16)
        bias = jnp.concatenate([b_group[i], b_sub[i]]).astype(F32)
        bias = jnp.pad(bias, (0, _R_LANES - bias.shape[0])).reshape(1, _R_LANES)
        hn, info, counts = _router(h, ffn_norm[i].reshape(1, D_MODEL).astype(F32), w_hi, w_lo, bias)
        dest, blk_e, n_valid, row_tok = _dispatch(info, counts)
        xs = hn.at[row_tok].get(mode='promise_in_bounds')
        ys = _moe_ffn(blk_e, n_valid, xs, w_gate_up, w_down, i)
        ma = ys.at[dest[0]].get(mode='promise_in_bounds')
        mb = ys.at[dest[1]].get(mode='promise_in_bounds')

        last = i == DEPTH - 1
        out = _ple(h, ma, mb, info, ple_norm[i].reshape(1, D_MODEL).astype(F32), w_ple_gate[i].astype(BF16),
                   pp, ps, w_ple_proj[i].astype(BF16), i, n_p, last)
        if last:
            y_p, y_s = out
        else:
            h = out

    kt_all, v_all = kv_prev
    k_prompt = kt_all.reshape(n_attn, batch, N_HEADS, 2, QK_DIM, seq).transpose(0, 1, 5, 2, 3, 4)
    v_prompt = v_all.reshape(n_attn, batch, seq, N_HEADS, V_DIM)
    return (y_p.reshape(batch, seq, D_MODEL), y_s.reshape(n_req, n_new, D_MODEL),
            k_prompt, v_prompt, jnp.stack(hl_p), jnp.stack(ct_p),
            jnp.stack(k_s), jnp.stack(v_s), jnp.stack(hl_s), jnp.stack(ct_s))
```

```python
import functools
import math

import jax
import jax.numpy as jnp
import numpy as np
from jax import lax
from jax.experimental import pallas as pl
from jax.experimental.pallas import tpu as pltpu

F32 = jnp.float32
BF16 = jnp.bfloat16

D_MODEL = 1024
DEPTH = 4
PAGE_SIZE = 128
N_HEADS = 8
QK_DIM = 64
V_DIM = 128
ROPE_DIM = 16
ROPE_THETA = 500000.0
D_RNN = 1280
RG_BLOCK = 128
N_RG_BLOCKS = 10
CONV_W = 4
RG_C = 8.0
N_GROUPS = 4
EXPERTS_PER_GROUP = 8
N_EXPERTS = 32
TOP_K = 2
D_EXPERT = 512
PLE_DIM = 256
EPS = 1e-6

TM = 256
TM_WIDE = 640
TQ = 512
TK = 256
PAGES_PER_STEP = 8
MOE_TM = 512
LRU_TS = 256
VMEM_LIMIT = 56 * 1024 * 1024

_NT = (((1,), (1,)), ((), ()))
_STRIP = 128
_Q_SCALE = QK_DIM ** -0.5 * math.log2(math.e)


def _cparams(sem):
    return pltpu.CompilerParams(dimension_semantics=sem, vmem_limit_bytes=VMEM_LIMIT)


def _rms(x, g):
    return x * lax.rsqrt(jnp.mean(x * x, axis=-1, keepdims=True) + EPS) * g


def _tile(i, n_p):
    return jnp.where(i == 0, n_p, i - 1)


def _ptile(i):
    return jnp.maximum(i - 1, 0)


def _qkv_kernel(h_ref, g_ref, wq_ref, wk_ref, wv_ref, qg_ref, kg_ref, cos_ref, sin_ref, *rest):
    qtb_ref, qs_ref, ktp_ref, kts_ref, kb_ref, vp_ref, vs_ref, vtb_ref, qt_s, kt_s = rest[-10:]
    i = pl.program_id(0)
    hn = _rms(h_ref[...], g_ref[...]).astype(BF16)
    cos = cos_ref[...]
    sin = sin_ref[...]

    def normrope(w_ref, gain_ref, dst, scale):
        xt = lax.dot_general(w_ref[...], hn, _NT, preferred_element_type=F32)
        gain = gain_ref[...]
        for g in range(2 * N_HEADS):
            xg = xt[g * QK_DIM:(g + 1) * QK_DIM, :]
            ss = jnp.sum(xg * xg, axis=0, keepdims=True)
            yg = xg * lax.rsqrt(ss * (1.0 / QK_DIM) + EPS) * gain
            y1 = yg[0:8]
            y2 = yg[8:16]
            dst[g * QK_DIM:g * QK_DIM + 8, :] = (y1 * cos - y2 * sin) * scale
            dst[g * QK_DIM + 8:g * QK_DIM + 16, :] = (y2 * cos + y1 * sin) * scale
            dst[g * QK_DIM + 16:(g + 1) * QK_DIM, :] = yg[16:] * scale

    normrope(wq_ref, qg_ref, qt_s, _Q_SCALE)
    normrope(wk_ref, kg_ref, kt_s, 1.0)
    v = jnp.dot(hn, wv_ref[...], preferred_element_type=F32)

    @pl.when(i == 0)
    def _():
        qs_ref[...] = qt_s[...].T
        kts_ref[...] = kt_s[...]
        vs_ref[...] = v

    @pl.when(i > 0)
    def _():
        kt = kt_s[...]
        qtb_ref[...] = qt_s[...].astype(BF16)
        ktp_ref[...] = kt
        kb_ref[...] = kt.T.astype(BF16)
        vp_ref[...] = v
        vtb_ref[...] = v.T.astype(BF16)


def _qkv(h, g, wq_t, wk_t, wv, qg_b, kg_b, cos_t, sin_t, n_p, batch, seq, layer, n_attn, prev):
    n_sb = seq // TM
    const = lambda i: (0, 0)
    tile = lambda i: (_tile(i, n_p), 0)
    ktile = lambda i: (_ptile(i) // n_sb, 0, _ptile(i) % n_sb)
    ltile = lambda i: (layer, _ptile(i) // n_sb, 0, _ptile(i) % n_sb)
    prev = () if prev is None else tuple(prev)
    n_in = 9
    return pl.pallas_call(
        _qkv_kernel,
        grid=(n_p + 1,),
        in_specs=[
            pl.BlockSpec((TM, D_MODEL), tile),
            pl.BlockSpec((1, D_MODEL), const),
            pl.BlockSpec((D_MODEL, D_MODEL), const),
            pl.BlockSpec((D_MODEL, D_MODEL), const),
            pl.BlockSpec((D_MODEL, D_MODEL), const),
            pl.BlockSpec((QK_DIM, TM), const),
            pl.BlockSpec((QK_DIM, TM), const),
            pl.BlockSpec((8, TM), lambda i: (0, _tile(i, n_p))),
            pl.BlockSpec((8, TM), lambda i: (0, _tile(i, n_p))),
        ] + [pl.BlockSpec(memory_space=pl.ANY)] * len(prev),
        out_specs=[
            pl.BlockSpec((None, D_MODEL, TM), ktile),
            pl.BlockSpec((TM, D_MODEL), const),
            pl.BlockSpec((None, None, D_MODEL, TM), ltile),
            pl.BlockSpec((D_MODEL, TM), const),
            pl.BlockSpec((TM, D_MODEL), lambda i: (_ptile(i), 0)),
            pl.BlockSpec((None, TM, D_MODEL), lambda i: (layer, _ptile(i), 0)),
            pl.BlockSpec((TM, D_MODEL), const),
            pl.BlockSpec((None, D_MODEL, TM), ktile),
        ],
        out_shape=[
            jax.ShapeDtypeStruct((batch, D_MODEL, seq), BF16),
            jax.ShapeDtypeStruct((TM, D_MODEL), F32),
            jax.ShapeDtypeStruct((n_attn, batch, D_MODEL, seq), F32),
            jax.ShapeDtypeStruct((D_MODEL, TM), F32),
            jax.ShapeDtypeStruct((n_p * TM, D_MODEL), BF16),
            jax.ShapeDtypeStruct((n_attn, n_p * TM, D_MODEL), F32),
            jax.ShapeDtypeStruct((TM, D_MODEL), F32),
            jax.ShapeDtypeStruct((batch, D_MODEL, seq), BF16),
        ],
        input_output_aliases={n_in: 2, n_in + 1: 5} if prev else {},
        scratch_shapes=[pltpu.VMEM((D_MODEL, TM), F32), pltpu.VMEM((D_MODEL, TM), F32)],
        compiler_params=_cparams(("arbitrary",)),
        name="qkv",
    )(h, g, wq_t, wk_t, wv, qg_b, kg_b, cos_t, sin_t, *prev)


def _resid_matmul_kernel(h_ref, xp_ref, xs_ref, w_ref, o_ref):
    i = pl.program_id(0)
    x = jnp.where(i == 0, xs_ref[...].astype(BF16), xp_ref[...].astype(BF16))
    o_ref[...] = h_ref[...] + jnp.dot(x, w_ref[...], preferred_element_type=F32)


def _resid_matmul(h, x_p, x_s, w, n_p):
    t_all = h.shape[0]
    k = w.shape[0]
    tile = lambda i: (_tile(i, n_p), 0)
    return pl.pallas_call(
        _resid_matmul_kernel,
        grid=(n_p + 1,),
        in_specs=[
            pl.BlockSpec((TM, D_MODEL), tile),
            pl.BlockSpec((TM, k), lambda i: (_ptile(i), 0)),
            pl.BlockSpec((TM, k), lambda i: (0, 0)),
            pl.BlockSpec((k, D_MODEL), lambda i: (0, 0)),
        ],
        out_specs=pl.BlockSpec((TM, D_MODEL), tile),
        out_shape=jax.ShapeDtypeStruct((t_all, D_MODEL), F32),
        compiler_params=_cparams(("arbitrary",)),
        name="resid_matmul",
    )(h, x_p, x_s, w)


def _norm_matmul_kernel(h_ref, g_ref, w_ref, o_ref):
    hn = _rms(h_ref[...], g_ref[...]).astype(BF16)
    o_ref[...] = jnp.dot(hn, w_ref[...], preferred_element_type=F32)


def _norm_matmul(h, g, w):
    t_all = h.shape[0]
    n = w.shape[1]
    return pl.pallas_call(
        _norm_matmul_kernel,
        grid=(t_all // TM_WIDE,),
        in_specs=[
            pl.BlockSpec((TM_WIDE, D_MODEL), lambda i: (i, 0)),
            pl.BlockSpec((1, D_MODEL), lambda i: (0, 0)),
            pl.BlockSpec((D_MODEL, n), lambda i: (0, 0)),
        ],
        out_specs=pl.BlockSpec((TM_WIDE, n), lambda i: (i, 0)),
        out_shape=jax.ShapeDtypeStruct((t_all, n), F32),
        compiler_params=_cparams(("arbitrary",)),
        name="norm_matmul",
    )(h, g, w)


_R_E1, _R_E2, _R_W1, _R_W2, _R_RANK1, _R_RANK2 = range(6)
_R_LANES = 128
_R_ROWS = 8
_EXPERT_LANE0 = N_GROUPS


def _router_kernel(h_ref, g_ref, whi_ref, wlo_ref, bias_ref, hn_ref, info_ref, cnt_ref, cnt_s):
    i = pl.program_id(0)

    @pl.when(i == 0)
    def _():
        cnt_s[...] = jnp.zeros_like(cnt_s)

    hn = _rms(h_ref[...], g_ref[...])
    hi = hn.astype(BF16)
    bits = pltpu.bitcast(hi.astype(F32), jnp.uint32)
    hn_ref[...] = (bits[:, :D_MODEL // 2] >> 16) | bits[:, D_MODEL // 2:]
    lo = (hn - hi.astype(F32)).astype(BF16)
    whi = whi_ref[...]
    lg = jnp.dot(hi, whi, preferred_element_type=F32)
    lg += jnp.dot(lo, whi, preferred_element_type=F32)
    lg += jnp.dot(hi, wlo_ref[...], preferred_element_type=F32)
    lg += bias_ref[...]

    lane = lax.broadcasted_iota(jnp.int32, lg.shape, 1).astype(F32)
    far = float(_R_LANES)
    rmax = lambda x: jnp.max(x, axis=-1, keepdims=True)
    rmin = lambda x: jnp.min(x, axis=-1, keepdims=True)
    rsum = lambda x: jnp.sum(x, axis=-1, keepdims=True)

    gl = jnp.where(lane < N_GROUPS, lg, -jnp.inf)
    gmax = rmax(gl)
    ge = jnp.exp(gl - gmax)
    gp = ge / rsum(ge)
    g_idx = rmin(jnp.where(gl == gmax, lane, far))
    g_w = rsum(jnp.where(lane == g_idx, gp, 0.0))

    lane0 = _EXPERT_LANE0 + g_idx * EXPERTS_PER_GROUP
    smask = (lane >= lane0) & (lane < lane0 + EXPERTS_PER_GROUP)
    sl = jnp.where(smask, lg, -jnp.inf)
    se = jnp.exp(sl - rmax(sl))
    sp = jnp.where(smask, se / rsum(se), -1.0)
    v1 = rmax(sp)
    i1 = rmin(jnp.where(sp == v1, lane, far))
    sp2 = jnp.where(lane == i1, -1.0, sp)
    v2 = rmax(sp2)
    i2 = rmin(jnp.where(sp2 == v2, lane, far))
    den = v1 + v2
    w1 = g_w * (v1 / den)
    w2 = g_w * (v2 / den)

    oh1 = jnp.where(lane == i1, 1.0, 0.0)
    oh2 = jnp.where(lane == i2, 1.0, 0.0)
    r = lax.broadcasted_iota(jnp.int32, (TM_WIDE, TM_WIDE), 0)
    c = lax.broadcasted_iota(jnp.int32, (TM_WIDE, TM_WIDE), 1)
    tri = jnp.where(c < r, 1.0, 0.0).astype(BF16)
    pre1 = jnp.dot(tri, oh1.astype(BF16), preferred_element_type=F32)
    pre2 = jnp.dot(tri, oh2.astype(BF16), preferred_element_type=F32)
    c1 = jnp.sum(oh1, axis=0, keepdims=True)
    c2 = jnp.sum(oh2, axis=0, keepdims=True)
    base = cnt_s[...]
    rank1 = rsum(oh1 * (pre1 + base))
    rank2 = rsum(oh2 * (pre2 + (base + c1)))
    total = base + (c1 + c2)
    cnt_s[...] = total
    cnt_ref[...] = total

    info = jnp.zeros_like(lg)
    for k, val in ((_R_E1, i1 - _EXPERT_LANE0), (_R_E2, i2 - _EXPERT_LANE0), (_R_W1, w1), (_R_W2, w2),
                   (_R_RANK1, rank1), (_R_RANK2, rank2)):
        info = jnp.where(lane == float(k), val, info)
    info_ref[...] = info.T[:_R_ROWS]


def _router(h, g, w_hi, w_lo, bias):
    t_all = h.shape[0]
    const = lambda i: (0, 0)
    return pl.pallas_call(
        _router_kernel,
        grid=(t_all // TM_WIDE,),
        in_specs=[
            pl.BlockSpec((TM_WIDE, D_MODEL), lambda i: (i, 0)),
            pl.BlockSpec((1, D_MODEL), const),
            pl.BlockSpec((D_MODEL, _R_LANES), const),
            pl.BlockSpec((D_MODEL, _R_LANES), const),
            pl.BlockSpec((1, _R_LANES), const),
        ],
        out_specs=[
            pl.BlockSpec((TM_WIDE, D_MODEL // 2), lambda i: (i, 0)),
            pl.BlockSpec((_R_ROWS, TM_WIDE), lambda i: (0, i)),
            pl.BlockSpec((1, _R_LANES), const),
        ],
        out_shape=[
            jax.ShapeDtypeStruct((t_all, D_MODEL // 2), jnp.uint32),
            jax.ShapeDtypeStruct((_R_ROWS, t_all), F32),
            jax.ShapeDtypeStruct((1, _R_LANES), F32),
        ],
        scratch_shapes=[pltpu.VMEM((1, _R_LANES), F32)],
        compiler_params=_cparams(("arbitrary",)),
        name="router",
    )(h, g, w_hi, w_lo, bias)


def _moe_ffn_kernel(be_ref, nv_ref, x_ref, wgu_ref, wd_ref, o_ref, wgu_s, wd_s):
    j = pl.program_id(0)
    valid = j < nv_ref[0]
    new_expert = (j == 0) | (be_ref[j] != be_ref[jnp.maximum(j - 1, 0)])

    @pl.when(valid & new_expert)
    def _():
        wgu_s[...] = wgu_ref[...].astype(BF16)
        wd_s[...] = wd_ref[...].astype(BF16)

    @pl.when(valid)
    def _():
        words = x_ref[...]
        x = jnp.concatenate([pltpu.bitcast(words << 16, F32),
                             pltpu.bitcast(words & jnp.uint32(0xFFFF0000), F32)], axis=1).astype(BF16)
        gu = jnp.dot(x, wgu_s[...], preferred_element_type=F32)
        act = jax.nn.silu(gu[:, :D_EXPERT]) * gu[:, D_EXPERT:]
        o_ref[...] = jnp.dot(act.astype(BF16), wd_s[...], preferred_element_type=F32)

    @pl.when(jnp.logical_not(valid))
    def _():
        o_ref[...] = jnp.zeros_like(o_ref)


def _moe_ffn(blk_e, n_valid, xs, wgu, wd, layer):
    n_rows = xs.shape[0]
    n_blk = n_rows // MOE_TM
    return pl.pallas_call(
        _moe_ffn_kernel,
        grid_spec=pltpu.PrefetchScalarGridSpec(
            num_scalar_prefetch=2,
            grid=(n_blk,),
            in_specs=[
                pl.BlockSpec((MOE_TM, D_MODEL // 2), lambda j, be, nv: (j, 0)),
                pl.BlockSpec((None, None, D_MODEL, 2 * D_EXPERT), lambda j, be, nv: (layer, be[j], 0, 0)),
                pl.BlockSpec((None, None, D_EXPERT, D_MODEL), lambda j, be, nv: (layer, be[j], 0, 0)),
            ],
            out_specs=pl.BlockSpec((MOE_TM, D_MODEL), lambda j, be, nv: (j, 0)),
            scratch_shapes=[pltpu.VMEM((D_MODEL, 2 * D_EXPERT), BF16), pltpu.VMEM((D_EXPERT, D_MODEL), BF16)],
        ),
        out_shape=jax.ShapeDtypeStruct((n_rows, D_MODEL), F32),
        compiler_params=_cparams(("arbitrary",)),
        name="moe_ffn",
    )(blk_e, n_valid, xs, wgu, wd)


def _ple_kernel(h_ref, ma_ref, mb_ref, info_ref, g_ref, wg_ref, pp_ref, ps_ref, wp_ref, *o_refs, split):
    i = pl.program_id(0)
    info = jnp.concatenate([info_ref[...], jnp.zeros((_R_LANES - _R_ROWS, TM), F32)], axis=0).T
    w1 = info[:, _R_W1:_R_W1 + 1]
    w2 = info[:, _R_W2:_R_W2 + 1]
    x = h_ref[...] + (w1 * ma_ref[...] + w2 * mb_ref[...])
    hn = _rms(x, g_ref[...]).astype(BF16)
    gate = jax.nn.sigmoid(jnp.dot(hn, wg_ref[...], preferred_element_type=F32))
    p = jnp.where(i == 0, ps_ref[...], pp_ref[...]).astype(BF16)
    out = x + gate * jnp.dot(p, wp_ref[...], preferred_element_type=F32)
    if split:
        yp_ref, ys_ref = o_refs

        @pl.when(i == 0)
        def _():
            ys_ref[...] = out

        @pl.when(i > 0)
        def _():
            yp_ref[...] = out
    else:
        o_refs[0][...] = out


def _ple(h, ma, mb, info, g, wg, p_p, p_s, wp, layer, n_p, split):
    t_all = h.shape[0]
    tile = lambda i: (_tile(i, n_p), 0)
    const = lambda i: (0, 0)
    if split:
        out_specs = [pl.BlockSpec((TM, D_MODEL), lambda i: (_ptile(i), 0)),
                     pl.BlockSpec((TM, D_MODEL), const)]
        out_shape = [jax.ShapeDtypeStruct((n_p * TM, D_MODEL), F32),
                     jax.ShapeDtypeStruct((TM, D_MODEL), F32)]
    else:
        out_specs = pl.BlockSpec((TM, D_MODEL), tile)
        out_shape = jax.ShapeDtypeStruct((t_all, D_MODEL), F32)
    return pl.pallas_call(
        functools.partial(_ple_kernel, split=split),
        grid=(n_p + 1,),
        in_specs=[
            pl.BlockSpec((TM, D_MODEL), tile),
            pl.BlockSpec((TM, D_MODEL), tile),
            pl.BlockSpec((TM, D_MODEL), tile),
            pl.BlockSpec((_R_ROWS, TM), lambda i: (0, _tile(i, n_p))),
            pl.BlockSpec((1, D_MODEL), const),
            pl.BlockSpec((D_MODEL, D_MODEL), const),
            pl.BlockSpec((None, TM, PLE_DIM), lambda i: (layer, _ptile(i), 0)),
            pl.BlockSpec((None, TM, PLE_DIM), lambda i: (layer, 0, 0)),
            pl.BlockSpec((PLE_DIM, D_MODEL), const),
        ],
        out_specs=out_specs,
        out_shape=out_shape,
        compiler_params=_cparams(("arbitrary",)),
        name="ple",
    )(h, ma, mb, info, g, wg, p_p, p_s, wp)


def _lambda(lp_ref, lambda_init):
    lp = lp_ref[...]
    s1 = jnp.sum(lp[0:1] * lp[1:2], axis=-1, keepdims=True)
    s2 = jnp.sum(lp[2:3] * lp[3:4], axis=-1, keepdims=True)
    return jnp.exp(s1) - jnp.exp(s2) + lambda_init


def _head_out(o0, o1, lam, sg, lambda_init):
    o = o0 - lam * o1
    return _rms(o, sg) * (1.0 - lambda_init)


def _flash_body(qi, qt_ref, k_ref, vt_ref, lp_ref, sg_ref, o_ref, acc_s, lambda_init):
    qt = qt_ref[...]
    zero = jnp.zeros((QK_DIM, TQ), qt.dtype)
    q2t = jnp.concatenate([jnp.concatenate([qt[:QK_DIM], zero], axis=0),
                           jnp.concatenate([zero, qt[QK_DIM:]], axis=0)], axis=1)
    acc_s[...] = jnp.zeros_like(acc_s)
    n_strip = 2 * TQ // _STRIP
    tiles_per_q = TQ // TK

    def attend(j, carry, diag):
        c0 = pl.multiple_of(j * TK, TK)
        kj = k_ref[pl.ds(c0, TK), :]
        vtj = vt_ref[:, pl.ds(c0, TK)]
        out = []
        for c in range(n_strip):
            cs = slice(c * _STRIP, (c + 1) * _STRIP)
            tok0 = (c * _STRIP) % TQ
            key0 = 0 if diag is None else diag * TK
            if diag is not None and key0 > tok0 + _STRIP - 1:
                out.append(carry[c])
                continue
            m_prev, l_prev = carry[c]
            st = jnp.dot(kj, q2t[:, cs], preferred_element_type=F32)
            if diag is not None and key0 + TK - 1 > tok0:
                key = lax.broadcasted_iota(jnp.int32, st.shape, 0) + key0
                tok = lax.broadcasted_iota(jnp.int32, st.shape, 1) + tok0
                st = jnp.where(key <= tok, st, -jnp.inf)
            m_new = jnp.maximum(m_prev, jnp.max(st, axis=0, keepdims=True))
            alpha = jnp.exp2(m_prev - m_new)
            p = jnp.exp2(st - m_new)
            l_new = alpha * l_prev + jnp.sum(p, axis=0, keepdims=True)
            pv = jnp.dot(vtj, p.astype(BF16), preferred_element_type=F32)
            acc_s[:, cs] = alpha * acc_s[:, cs] + pv
            out.append((m_new, l_new))
        return tuple(out)

    def group(jj, carry):
        for u in range(tiles_per_q):
            carry = attend(tiles_per_q * jj + u, carry, None)
        return carry

    init = tuple((jnp.full((1, _STRIP), -jnp.inf, F32), jnp.zeros((1, _STRIP), F32)) for _ in range(n_strip))
    carry = lax.fori_loop(0, qi, group, init)
    for d in range(tiles_per_q):
        carry = attend(tiles_per_q * qi + d, carry, d)

    on = acc_s[...] * (1.0 / jnp.concatenate([l for _, l in carry], axis=1))
    lam = _lambda(lp_ref, lambda_init)
    ot = on[:, :TQ] - lam * on[:, TQ:]
    y = ot * lax.rsqrt(jnp.mean(ot * ot, axis=0, keepdims=True) + EPS) * sg_ref[...]
    o_ref[...] = (y * (1.0 - lambda_init)).T.astype(o_ref.dtype)


def _attn_kernel(pt_ref, qt_ref, k_ref, vt_ref, lp_ref, sgb_ref, q_ref, ktn_ref, vn_ref, sg_ref, *rest,
                 lambda_init, n_new):
    del pt_ref
    npg = PAGES_PER_STEP
    k_refs = rest[:npg]
    v_refs = rest[npg:2 * npg]
    op_ref, os_ref = rest[2 * npg:2 * npg + 2]
    acc_p, q2_s, m_s, l_s, acc_s = rest[2 * npg + 2:]
    step = pl.program_id(2)
    _decode_body(step, pl.num_programs(2), q_ref, ktn_ref, vn_ref, lp_ref, sg_ref, k_refs, v_refs, os_ref,
                 q2_s, m_s, l_s, acc_s, lambda_init, n_new)
    _flash_body(step, qt_ref, k_ref, vt_ref, lp_ref, sgb_ref, op_ref, acc_p, lambda_init)


def _attention(pt_flat, qt_b, k_b, vt_b, lam_p, sub_g_b, q_s, ktn, v_s, sub_g, ck_t, cv_r, layer,
               batch, seq, n_req, n_new, n_pages, lambda_init):
    npg = PAGES_PER_STEP
    n_q = seq // TQ
    assert n_req == batch * N_HEADS and n_pages == n_q * npg
    n_cols = 2 * N_HEADS * n_new
    const = lambda b, h, i, pt: (0, 0)
    req = lambda b, h: b * N_HEADS + h

    def page_spec(g):
        return pl.BlockSpec((None, None, D_MODEL, PAGE_SIZE),
                            lambda b, h, i, pt: (layer, pt[req(b, h) * n_pages + i * npg + g], 0, 0))

    return pl.pallas_call(
        functools.partial(_attn_kernel, lambda_init=lambda_init, n_new=n_new),
        grid_spec=pltpu.PrefetchScalarGridSpec(
            num_scalar_prefetch=1,
            grid=(batch, N_HEADS, n_q),
            in_specs=[
                pl.BlockSpec((None, V_DIM, TQ), lambda b, h, i, pt: (b, h, i)),
                pl.BlockSpec((seq, V_DIM), lambda b, h, i, pt: (b, h)),
                pl.BlockSpec((None, V_DIM, seq), lambda b, h, i, pt: (b, h, 0)),
                pl.BlockSpec((4, QK_DIM), const),
                pl.BlockSpec((V_DIM, TQ), const),
                pl.BlockSpec((n_new, D_MODEL), lambda b, h, i, pt: (req(b, h), 0)),
                pl.BlockSpec((None, D_MODEL, PAGE_SIZE), lambda b, h, i, pt: (req(b, h), 0, 0)),
                pl.BlockSpec((n_new, D_MODEL), lambda b, h, i, pt: (req(b, h), 0)),
                pl.BlockSpec((1, V_DIM), const),
            ] + [page_spec(g) for g in range(npg)] + [page_spec(g) for g in range(npg)],
            out_specs=[
                pl.BlockSpec((TQ, V_DIM), lambda b, h, i, pt: (b * n_q + i, h)),
                pl.BlockSpec((n_new, D_MODEL), lambda b, h, i, pt: (req(b, h), 0)),
            ],
            scratch_shapes=[pltpu.VMEM((V_DIM, 2 * TQ), F32),
                            pltpu.VMEM((n_cols, D_MODEL), BF16), pltpu.VMEM((n_cols, 1), F32),
                            pltpu.VMEM((n_cols, 1), F32), pltpu.VMEM((n_cols, D_MODEL), F32)],
        ),
        out_shape=[jax.ShapeDtypeStruct((batch * seq, D_MODEL), BF16),
                   jax.ShapeDtypeStruct((n_req * n_new, D_MODEL), F32)],
        compiler_params=_cparams(("arbitrary", "arbitrary", "arbitrary")),
        name="attention",
    )(pt_flat, qt_b, k_b, vt_b, lam_p, sub_g_b, q_s, ktn, v_s, sub_g, *([ck_t] * npg), *([cv_r] * npg))


def _decode_body(p, n_steps, q_ref, ktn_ref, vn_ref, lp_ref, sg_ref, k_refs, v_refs, o_ref,
                 q2_s, m_s, l_s, acc_s, lambda_init, n_new):
    npg = len(k_refs)

    @pl.when(p == 0)
    def _():
        qq = jnp.concatenate([q_ref[...]] * (2 * N_HEADS), axis=0)
        row = lax.broadcasted_iota(jnp.int32, qq.shape, 0)
        lane = lax.broadcasted_iota(jnp.int32, qq.shape, 1)
        q2_s[...] = jnp.where(lane // QK_DIM == row // n_new, qq, 0.0).astype(BF16)
        m_s[...] = jnp.full_like(m_s, -jnp.inf)
        l_s[...] = jnp.zeros_like(l_s)
        acc_s[...] = jnp.zeros_like(acc_s)

    def update(s, v):
        m_prev = m_s[...]
        m_new = jnp.maximum(m_prev, jnp.max(s, axis=-1, keepdims=True))
        alpha = jnp.exp2(m_prev - m_new)
        pr = jnp.exp2(s - m_new)
        l_s[...] = alpha * l_s[...] + jnp.sum(pr, axis=-1, keepdims=True)
        acc_s[...] = alpha * acc_s[...] + jnp.dot(pr.astype(BF16), v, preferred_element_type=F32)
        m_s[...] = m_new

    q2 = q2_s[...]
    s = jnp.concatenate(
        [jnp.dot(q2, k_refs[g][...].astype(BF16), preferred_element_type=F32) for g in range(npg)], axis=1)
    v = jnp.concatenate(
        [jnp.concatenate([v_refs[g][pl.ds(h, PAGE_SIZE, stride=N_HEADS), :] for h in range(N_HEADS)], axis=1)
         for g in range(npg)], axis=0).astype(BF16)
    update(s, v)

    @pl.when(p == n_steps - 1)
    def _():
        sn = jnp.dot(q2, ktn_ref[...].astype(BF16), preferred_element_type=F32)
        row = lax.broadcasted_iota(jnp.int32, sn.shape, 0)
        col = lax.broadcasted_iota(jnp.int32, sn.shape, 1)
        sn = jnp.where(col <= row % n_new, sn, -jnp.inf)
        vn = jnp.concatenate([vn_ref[...], jnp.zeros((PAGE_SIZE - n_new, D_MODEL), F32)], axis=0).astype(BF16)
        update(sn, vn)
        lam = _lambda(lp_ref, lambda_init)
        on = acc_s[...] / l_s[...]
        for h in range(N_HEADS):
            blk = on[h * 2 * n_new:(h + 1) * 2 * n_new, h * V_DIM:(h + 1) * V_DIM]
            o_ref[:, h * V_DIM:(h + 1) * V_DIM] = _head_out(blk[:n_new], blk[n_new:], lam, sg_ref[...],
                                                             lambda_init)


def _lru_kernel(u_ref, buf_ref, h0_ref, cw_ref, cb_ref, wa_ref, ba_ref, wi_ref, bi_ref, lam_ref,
                y_ref, hl_ref, ct_ref, xbuf, a_s, b_s, hc, *, ts):
    t = pl.program_id(1)

    @pl.when(t == 0)
    def _():
        xbuf[0:8, :] = buf_ref[...]
        hc[...] = jnp.broadcast_to(h0_ref[...], (8, D_RNN))

    xr = u_ref[:, D_RNN:]
    xbuf[8:8 + ts, :] = xr
    cw = cw_ref[...]
    xc = cb_ref[...] + (cw[0:1] * xbuf[5:5 + ts, :] + cw[1:2] * xbuf[6:6 + ts, :]
                        + cw[2:3] * xbuf[7:7 + ts, :] + cw[3:4] * xr)
    z = -lam_ref[...]
    sp = jnp.maximum(z, 0.0) + jnp.log1p(jnp.exp(-jnp.abs(z)))
    row = lax.broadcasted_iota(jnp.int32, (ts, RG_BLOCK), 0) % 8
    for n in range(N_RG_BLOCKS):
        sl = slice(n * RG_BLOCK, (n + 1) * RG_BLOCK)
        xcn = xc[:, sl]
        xb = xcn.astype(BF16)
        r = jax.nn.sigmoid(jnp.dot(xb, wa_ref[n], preferred_element_type=F32) + ba_ref[:, sl])
        ig = jax.nn.sigmoid(jnp.dot(xb, wi_ref[n], preferred_element_type=F32) + bi_ref[:, sl])
        log_a = -RG_C * r * sp[:, sl]
        a = jnp.exp(log_a)
        mult = jnp.sqrt(-jnp.tanh(log_a) * (a * a + 1.0))
        b = mult * (ig * xcn)
        for s in (1, 2, 4):
            a_sh = pltpu.roll(a, s, 0)
            b_sh = pltpu.roll(b, s, 0)
            keep = row >= s
            b = jnp.where(keep, a * b_sh + b, b)
            a = jnp.where(keep, a * a_sh, a)
        a_s[:, sl] = a
        b_s[:, sl] = b

    def step(i, h):
        r0 = pl.multiple_of(i * 8, 8)
        ht = a_s[pl.ds(r0, 8), :] * h + b_s[pl.ds(r0, 8), :]
        b_s[pl.ds(r0, 8), :] = ht
        return jnp.broadcast_to(ht[7:8, :], (8, D_RNN))

    h = lax.fori_loop(0, ts // 8, step, hc[...])
    hc[...] = h
    xbuf[0:8, :] = xbuf[ts:ts + 8, :]
    y_ref[...] = (b_s[...] * jax.nn.gelu(u_ref[:, :D_RNN])).astype(y_ref.dtype)

    @pl.when(t == pl.num_programs(1) - 1)
    def _():
        hl_ref[...] = h[0:1]
        ct_ref[...] = xbuf[0:8, :]


def _lru(u, buf0, h0, cw, cb, wa, ba, wi, bi, lam, n_seq, seq, ts, row_blk0, y_dtype):
    nt = seq // ts
    c2 = lambda b, t: (0, 0)
    c3 = lambda b, t: (0, 0, 0)
    return pl.pallas_call(
        functools.partial(_lru_kernel, ts=ts),
        grid=(n_seq, nt),
        in_specs=[
            pl.BlockSpec((ts, 2 * D_RNN), lambda b, t: (row_blk0 + b * nt + t, 0)),
            pl.BlockSpec((None, 8, D_RNN), lambda b, t: (b, 0, 0)),
            pl.BlockSpec((None, 1, D_RNN), lambda b, t: (b, 0, 0)),
            pl.BlockSpec((CONV_W, D_RNN), c2),
            pl.BlockSpec((1, D_RNN), c2),
            pl.BlockSpec((N_RG_BLOCKS, RG_BLOCK, RG_BLOCK), c3),
            pl.BlockSpec((1, D_RNN), c2),
            pl.BlockSpec((N_RG_BLOCKS, RG_BLOCK, RG_BLOCK), c3),
            pl.BlockSpec((1, D_RNN), c2),
            pl.BlockSpec((1, D_RNN), c2),
        ],
        out_specs=[
            pl.BlockSpec((ts, D_RNN), lambda b, t: (b * nt + t, 0)),
            pl.BlockSpec((None, 1, D_RNN), lambda b, t: (b, 0, 0)),
            pl.BlockSpec((None, 8, D_RNN), lambda b, t: (b, 0, 0)),
        ],
        out_shape=[
            jax.ShapeDtypeStruct((n_seq * seq, D_RNN), y_dtype),
            jax.ShapeDtypeStruct((n_seq, 1, D_RNN), F32),
            jax.ShapeDtypeStruct((n_seq, 8, D_RNN), F32),
        ],
        scratch_shapes=[pltpu.VMEM((ts + 8, D_RNN), F32), pltpu.VMEM((ts, D_RNN), F32),
                        pltpu.VMEM((ts, D_RNN), F32), pltpu.VMEM((8, D_RNN), F32)],
        compiler_params=_cparams(("arbitrary", "arbitrary")),
        name="lru",
    )(u, buf0, h0, cw, cb, wa, ba, wi, bi, lam)


def _dispatch(info, counts_row):
    t_all = info.shape[1]
    eid = info[_R_E1:_R_E2 + 1].astype(jnp.int32)
    rank = info[_R_RANK1:_R_RANK2 + 1].astype(jnp.int32)
    counts = counts_row[0, _EXPERT_LANE0:_EXPERT_LANE0 + N_EXPERTS].astype(jnp.int32)
    pcounts = ((counts + MOE_TM - 1) // MOE_TM) * MOE_TM
    pends = jnp.cumsum(pcounts)
    pstarts = pends - pcounts
    experts = jnp.arange(N_EXPERTS, dtype=jnp.int32)[:, None, None]
    start = jnp.sum(jnp.where(eid[None] == experts, pstarts[:, None, None], 0), axis=0)
    dest = start + rank
    n_blk = (t_all * TOP_K) // MOE_TM + N_EXPERTS
    blk_row0 = jnp.arange(n_blk, dtype=jnp.int32) * MOE_TM
    blk_e = jnp.minimum(jnp.sum((pends[None, :] <= blk_row0[:, None]).astype(jnp.int32), axis=1), N_EXPERTS - 1)
    n_valid = (pends[-1] // MOE_TM).astype(jnp.int32).reshape(1)
    tok = jnp.broadcast_to(jnp.arange(t_all, dtype=jnp.int32)[None, :], (TOP_K, t_all))
    row_tok = (jnp.arange(n_blk * MOE_TM, dtype=jnp.int32) % t_all).at[dest.reshape(-1)].set(
        tok.reshape(-1), mode='promise_in_bounds', unique_indices=True)
    return dest, blk_e.astype(jnp.int32), n_valid, row_tok


def kernel(x_prompt, x_sample, cache_k, cache_v, state_h, state_conv, page_table, p_prompt, p_sample, mix_norm, ffn_norm, ple_norm, w_qkv, q_norm, k_norm, lambda_q1, lambda_k1, lambda_q2, lambda_k2, sub_norm, w_o, w_in_lru, conv_w, conv_b, w_rg_a, b_rg_a, w_rg_i, b_rg_i, lru_lambda, w_out_lru, w_group, b_group, w_sub, b_sub, w_gate_up, w_down, w_ple_gate, w_ple_proj):
    batch, seq, _ = x_prompt.shape
    n_req, n_new, _ = x_sample.shape
    n_pages = page_table.shape[1]
    past_len = n_pages * PAGE_SIZE
    t_p = batch * seq
    t_s = n_req * n_new
    assert t_s == TM and t_p % TM == 0 and seq % TM == 0 and n_pages % PAGES_PER_STEP == 0
    assert (t_p + t_s) % TM_WIDE == 0 and seq % TQ == 0
    n_p = t_p // TM
    n_attn = cache_k.shape[0]
    n_pool = cache_k.shape[1]

    h = jnp.concatenate([x_prompt.reshape(t_p, D_MODEL), x_sample.reshape(t_s, D_MODEL)], axis=0)

    pos = jnp.concatenate([jnp.tile(jnp.arange(seq, dtype=jnp.int32), batch),
                           jnp.tile(past_len + jnp.arange(n_new, dtype=jnp.int32), n_req)])
    half = ROPE_DIM // 2
    inv = ROPE_THETA ** (-(jnp.arange(half, dtype=F32) * 2.0 / ROPE_DIM))
    ang = pos.astype(F32)[:, None] * inv[None, :]
    cos_t = jnp.cos(ang).T
    sin_t = jnp.sin(ang).T

    ck_t = cache_k.transpose(0, 1, 3, 4, 5, 2).reshape(n_attn, n_pool, D_MODEL, PAGE_SIZE)
    cv_r = cache_v.reshape(n_attn, n_pool, PAGE_SIZE * N_HEADS, V_DIM)
    pt_flat = page_table.reshape(-1).astype(jnp.int32)
    pp = p_prompt.reshape(DEPTH, t_p, PLE_DIM)
    ps = p_sample.reshape(DEPTH, t_s, PLE_DIM)

    k_s, v_s, hl_p, ct_p, hl_s, ct_s = [], [], [], [], [], []
    y_p = y_s = kv_prev = None
    for i in range(DEPTH):
        j = i // 2
        g_mix = mix_norm[i].reshape(1, D_MODEL).astype(F32)
        if i % 2 == 0:
            lambda_init = 0.8 - 0.6 * math.exp(-0.3 * i)
            wq_t = w_qkv[j, :, :D_MODEL].T.astype(BF16)
            wk_t = w_qkv[j, :, D_MODEL:2 * D_MODEL].T.astype(BF16)
            wv = w_qkv[j, :, 2 * D_MODEL:].astype(BF16)
            qg_b = jnp.broadcast_to(q_norm[j].astype(F32)[:, None], (QK_DIM, TM))
            kg_b = jnp.broadcast_to(k_norm[j].astype(F32)[:, None], (QK_DIM, TM))
            qt_b, q_s, kt_p, kt_s, k_b, vv_p, vv_s, vt_b = _qkv(h, g_mix, wq_t, wk_t, wv, qg_b, kg_b, cos_t, sin_t,
                                                                n_p, batch, seq, j, n_attn, kv_prev)
            kv_prev = (kt_p, vv_p)
            lam_p = jnp.stack([lambda_q1[j], lambda_k1[j], lambda_q2[j], lambda_k2[j]]).astype(F32)
            sub_g = sub_norm[j].reshape(1, V_DIM).astype(F32)
            sub_g_b = jnp.broadcast_to(sub_norm[j].astype(F32)[:, None], (V_DIM, TQ))
            ktn = kt_s.reshape(D_MODEL, n_req, n_new).transpose(1, 0, 2)
            ktn = jnp.pad(ktn, ((0, 0), (0, 0), (0, PAGE_SIZE - n_new)))
            o_p, o_s = _attention(pt_flat, qt_b, k_b, vt_b, lam_p, sub_g_b, q_s, ktn, vv_s, sub_g, ck_t, cv_r, j,
                                  batch, seq, n_req, n_new, n_pages, lambda_init)
            h = _resid_matmul(h, o_p, o_s, w_o[j].astype(BF16), n_p)
            k_s.append(kt_s.T.reshape(n_req, n_new, N_HEADS, 2, QK_DIM))
            v_s.append(vv_s.reshape(n_req, n_new, N_HEADS, V_DIM))
        else:
            u = _norm_matmul(h, g_mix, w_in_lru[j].astype(BF16))
            cw = conv_w[j].astype(F32)
            cb = conv_b[j].reshape(1, D_RNN).astype(F32)
            wa = w_rg_a[j].astype(BF16)
            wi = w_rg_i[j].astype(BF16)
            ba = b_rg_a[j].reshape(1, D_RNN).astype(F32)
            bi = b_rg_i[j].reshape(1, D_RNN).astype(F32)
            lam = lru_lambda[j].reshape(1, D_RNN).astype(F32)
            zb = jnp.zeros((batch, 8, D_RNN), F32)
            zh = jnp.zeros((batch, 1, D_RNN), F32)
            yl_p, hlp, ctp = _lru(u, zb, zh, cw, cb, wa, ba, wi, bi, lam, batch, seq, LRU_TS, 0, BF16)
            buf_s = jnp.pad(state_conv[j].astype(F32), ((0, 0), (8 - (CONV_W - 1), 0), (0, 0)))
            h0_s = state_h[j].astype(F32)[:, None, :]
            yl_s, hls, cts = _lru(u, buf_s, h0_s, cw, cb, wa, ba, wi, bi, lam, n_req, n_new, n_new,
                                  t_p // n_new, F32)
            h = _resid_matmul(h, yl_p, yl_s, w_out_lru[j].astype(BF16), n_p)
            hl_p.append(hlp[:, 0])
            ct_p.append(ctp[:, 8 - (CONV_W - 1):])
            hl_s.append(hls[:, 0])
            ct_s.append(cts[:, 8 - (CONV_W - 1):])

        w_r = jnp.concatenate([w_group[i], w_sub[i]], axis=1).astype(F32)
        w_r = jnp.pad(w_r, ((0, 0), (0, 128 - w_r.shape[1])))
        w_hi = w_r.astype(BF16)
        w_lo = (w_r - w_hi.astype(F32)).astype(BF16)
        bias = jnp.concatenate([b_group[i], b_sub[i]]).astype(F32)
        bias = jnp.pad(bias, (0, _R_LANES - bias.shape[0])).reshape(1, _R_LANES)
        hn, info, counts = _router(h, ffn_norm[i].reshape(1, D_MODEL).astype(F32), w_hi, w_lo, bias)
        dest, blk_e, n_valid, row_tok = _dispatch(info, counts)
        xs = hn.at[row_tok].get(mode='promise_in_bounds')
        ys = _moe_ffn(blk_e, n_valid, xs, w_gate_up, w_down, i)
        ma = ys.at[dest[0]].get(mode='promise_in_bounds')
        mb = ys.at[dest[1]].get(mode='promise_in_bounds')

        last = i == DEPTH - 1
        out = _ple(h, ma, mb, info, ple_norm[i].reshape(1, D_MODEL).astype(F32), w_ple_gate[i].astype(BF16),
                   pp, ps, w_ple_proj[i].astype(BF16), i, n_p, last)
        if last:
            y_p, y_s = out
        else:
            h = out

    kt_all, v_all = kv_prev
    k_prompt = kt_all.reshape(n_attn, batch, N_HEADS, 2, QK_DIM, seq).transpose(0, 1, 5, 2, 3, 4)
    v_prompt = v_all.reshape(n_attn, batch, seq, N_HEADS, V_DIM)
    return (y_p.reshape(batch, seq, D_MODEL), y_s.reshape(n_req, n_new, D_MODEL),
            k_prompt, v_prompt, jnp.stack(hl_p), jnp.stack(ct_p),
            jnp.stack(k_s), jnp.stack(v_s), jnp.stack(hl_s), jnp.stack(ct_s))
```
